```python
import jax, jax.numpy as jnp
from jax import lax
import numpy as np

D_MODEL = 1024
BATCH = 4
SEQ = 4096
DEPTH = 4

HG_HEADS = 8
HG_HEAD_DIM = 128
HG_WIDTH = HG_HEADS * HG_HEAD_DIM
HG_CHUNK = 64
LRU_WIDTH = D_MODEL
LRU_HEADS = 8
LRU_HEAD_DIM = LRU_WIDTH // LRU_HEADS
LRU_CONV = 4
LRU_C = 8.0
CONV_WIDTH = D_MODEL
CONV_KERNEL = 31
N_BRANCH = 3
D_FF = 2816
MIX_COLS = (4 * HG_WIDTH) + (2 * LRU_WIDTH) + (2 * CONV_WIDTH) + (N_BRANCH * D_MODEL)

kernel_name = "hybrid_hgrn2_rglru_conformer_macaron"


def rmsnorm(x, g, eps=1e-6):
    xf = x.astype(jnp.float32)
    y = xf * lax.rsqrt(jnp.mean(xf * xf, axis=-1, keepdims=True) + eps)
    return (y * g.astype(jnp.float32)).astype(x.dtype)


def layernorm(x, g, b, eps=1e-5):
    xf = x.astype(jnp.float32)
    mu = jnp.mean(xf, axis=-1, keepdims=True)
    xc = xf - mu
    y = xc * lax.rsqrt(jnp.mean(xc * xc, axis=-1, keepdims=True) + eps)
    return (y * g.astype(jnp.float32) + b.astype(jnp.float32)).astype(x.dtype)


def swiglu_ffn(x, w_in, w_out):
    g, u = jnp.split(x @ w_in, 2, axis=-1)
    return (jax.nn.silu(g) * u) @ w_out


def causal_dwconv(x, w, b):
    k = w.shape[0]
    xp = jnp.pad(x, ((0, 0), (k - 1, 0), (0, 0)))
    y = lax.conv_general_dilated(
        xp, w[:, None, :].astype(x.dtype), window_strides=(1,), padding='VALID',
        dimension_numbers=('NWC', 'WIO', 'NWC'), feature_group_count=x.shape[-1])
    return y + b


def hgrn2_chunked(q, k, v, log_f):
    b_, s_, h_, dk = q.shape
    dv = v.shape[-1]
    n = s_ // HG_CHUNK

    def to_chunks(t):
        return t.reshape(b_, n, HG_CHUNK, h_, t.shape[-1]).transpose(1, 0, 3, 2, 4)

    qc, kc, vc, gc = to_chunks(q), to_chunks(k), to_chunks(v), to_chunks(log_f)
    causal = jnp.tril(jnp.ones((HG_CHUNK, HG_CHUNK), dtype=bool))[:, :, None]

    def step(state, inp):
        qb, kb, vb, gb = inp
        bcum = jnp.cumsum(gb, axis=2)
        diff = bcum[:, :, :, None, :] - bcum[:, :, None, :, :]
        decay = jnp.exp(jnp.where(causal, diff, -jnp.inf))
        scores = jnp.einsum('bhtk,bhsk,bhtsk->bhts', qb, kb, decay)
        o = (jnp.einsum('bhts,bhsv->bhtv', scores, vb)
             + jnp.einsum('bhtk,bhkv->bhtv', qb * jnp.exp(bcum), state))
        b_last = bcum[:, :, -1:, :]
        state = (jnp.exp(b_last[:, :, 0, :])[..., None] * state
                 + jnp.einsum('bhsk,bhsv->bhkv', kb * jnp.exp(b_last - bcum), vb))
        return state, o

    s0 = jnp.zeros((b_, h_, dk, dv), jnp.float32)
    _, o = lax.scan(step, s0, (qc, kc, vc, gc))
    return o.transpose(1, 0, 3, 2, 4).reshape(b_, s_, h_, dv)


def linear_scan(a, u):
    def combine(c1, c2):
        a1, b1 = c1
        a2, b2 = c2
        return a1 * a2, a2 * b1 + b2
    _, h = lax.associative_scan(combine, (a, u), axis=1)
    return h


def hybrid_mixer(u, lb, w_in, b_in, hg_norm, lru_conv_w, lru_conv_b, lru_gate_w, lru_gate_b,
                 lru_lambda, cv_dw_w, cv_dw_b, cv_ln_g, cv_ln_b, w_branch, w_out):
    bsz, s_, _ = u.shape
    f32 = jnp.float32
    proj = u @ w_in + b_in
    sizes = [HG_WIDTH] * 4 + [LRU_WIDTH] * 2 + [CONV_WIDTH] * 2 + [D_MODEL] * N_BRANCH
    cuts = [int(c) for c in np.cumsum(sizes)[:-1]]
    hq, hf, hi, hg, lx, lg, ca, cb, g_a, g_b, g_c = jnp.split(proj, cuts, axis=-1)

    hshape = (bsz, s_, HG_HEADS, HG_HEAD_DIM)
    q = jax.nn.silu(hq.astype(f32)).reshape(hshape)
    z = hf.astype(f32).reshape(hshape)
    lb_h = lb.astype(f32).reshape(HG_HEADS, HG_HEAD_DIM)
    f = lb_h + (1.0 - lb_h) * jax.nn.sigmoid(z)
    k = (1.0 - lb_h) * jax.nn.sigmoid(-z)
    v = hi.astype(f32).reshape(hshape)
    o_hg = hgrn2_chunked(q, k, v, jnp.log(f))
    y_hg = rmsnorm(o_hg, hg_norm).reshape(bsz, s_, HG_WIDTH).astype(u.dtype) * jax.nn.silu(hg)

    xb = causal_dwconv(lx, lru_conv_w, lru_conv_b)
    xh = xb.reshape(bsz, s_, LRU_HEADS, LRU_HEAD_DIM)
    gates = jnp.einsum('bshi,ghij->gbshj', xh, lru_gate_w).reshape(2, bsz, s_, LRU_WIDTH)
    gates = gates.astype(f32) + lru_gate_b.astype(f32)[:, None, None, :]
    r_t = jax.nn.sigmoid(gates[0])
    i_t = jax.nn.sigmoid(gates[1])
    log_a = -LRU_C * r_t * jax.nn.softplus(-lru_lambda.astype(f32))
    a_t = jnp.exp(log_a)
    mult = jnp.sqrt(-jnp.expm1(2.0 * log_a))
    h = linear_scan(a_t, mult * (i_t * xb.astype(f32)))
    y_lru = h.astype(u.dtype) * jax.nn.gelu(lg)

    cu = ca * jax.nn.sigmoid(cb)
    cu = causal_dwconv(cu, cv_dw_w, cv_dw_b)
    y_cv = jax.nn.silu(layernorm(cu, cv_ln_g, cv_ln_b))

    merged = (jax.nn.sigmoid(g_a) * (y_hg @ w_branch[0])
              + jax.nn.sigmoid(g_b) * (y_lru @ w_branch[1])
              + jax.nn.sigmoid(g_c) * (y_cv @ w_branch[2]))
    return merged @ w_out


def setup_inputs(seed: int = 0) -> dict:
    key = jax.random.key(seed)
    ks = jax.random.split(key, 24)
    L, D = DEPTH, D_MODEL

    def nrm(k, shape, fan_in):
        return jax.random.normal(k, shape, jnp.float32) * (fan_in ** -0.5)

    def gain(k, shape):
        return 1.0 + 0.01 * jax.random.normal(k, shape, jnp.float32)

    a_c = jax.random.uniform(ks[14], (L, LRU_WIDTH), jnp.float32, minval=0.9, maxval=0.999)
    sig = a_c ** (1.0 / LRU_C)
    lru_lambda = jnp.log(sig) - jnp.log1p(-sig)

    return {
        "x": jax.random.normal(ks[0], (BATCH, SEQ, D), jnp.float32),
        "norm_ffn1": gain(ks[1], (L, D)),
        "ffn1_w_in": nrm(ks[2], (L, D, 2 * D_FF), D),
        "ffn1_w_out": nrm(ks[3], (L, D_FF, D), D_FF),
        "norm_mix": gain(ks[4], (L, D)),
        "w_in_mix": nrm(ks[5], (L, D, MIX_COLS), D),
        "b_in_mix": 0.01 * jax.random.normal(ks[6], (L, MIX_COLS), jnp.float32),
        "hgrn_lb_logits": 0.5 * jax.random.normal(ks[7], (L, HG_WIDTH), jnp.float32),
        "hg_norm": gain(ks[8], (L, HG_HEAD_DIM)),
        "lru_conv_w": nrm(ks[9], (L, LRU_CONV, LRU_WIDTH), LRU_CONV),
        "lru_conv_b": 0.01 * jax.random.normal(ks[10], (L, LRU_WIDTH), jnp.float32),
        "lru_gate_w": nrm(ks[11], (L, 2, LRU_HEADS, LRU_HEAD_DIM, LRU_HEAD_DIM), LRU_HEAD_DIM),
        "lru_gate_b": 0.01 * jax.random.normal(ks[12], (L, 2, LRU_WIDTH), jnp.float32),
        "lru_lambda": lru_lambda,
        "cv_dw_w": nrm(ks[15], (L, CONV_KERNEL, CONV_WIDTH), CONV_KERNEL),
        "cv_dw_b": 0.01 * jax.random.normal(ks[16], (L, CONV_WIDTH), jnp.float32),
        "cv_ln_g": gain(ks[17], (L, CONV_WIDTH)),
        "cv_ln_b": 0.01 * jax.random.normal(ks[18], (L, CONV_WIDTH), jnp.float32),
        "w_branch": nrm(ks[19], (L, N_BRANCH, D, D), D),
        "w_out_mix": nrm(ks[20], (L, D, D), D),
        "norm_ffn2": gain(ks[21], (L, D)),
        "ffn2_w_in": nrm(ks[22], (L, D, 2 * D_FF), D),
        "ffn2_w_out": nrm(ks[23], (L, D_FF, D), D_FF),
        "norm_final": gain(ks[13], (D,)),
    }


def reference(x, norm_ffn1, ffn1_w_in, ffn1_w_out, norm_mix, w_in_mix, b_in_mix, hgrn_lb_logits,
              hg_norm, lru_conv_w, lru_conv_b, lru_gate_w, lru_gate_b, lru_lambda, cv_dw_w, cv_dw_b,
              cv_ln_g, cv_ln_b, w_branch, w_out_mix, norm_ffn2, ffn2_w_in, ffn2_w_out, norm_final):
    lb_all = jax.nn.softmax(hgrn_lb_logits.astype(jnp.float32), axis=0)
    lb_all = jnp.cumsum(lb_all, axis=0) - lb_all[:1]
    for l in range(DEPTH):
        x = x + 0.5 * swiglu_ffn(rmsnorm(x, norm_ffn1[l]), ffn1_w_in[l], ffn1_w_out[l])
        x = x + hybrid_mixer(rmsnorm(x, norm_mix[l]), lb_all[l], w_in_mix[l], b_in_mix[l],
                             hg_norm[l], lru_conv_w[l], lru_conv_b[l], lru_gate_w[l], lru_gate_b[l],
                             lru_lambda[l], cv_dw_w[l], cv_dw_b[l], cv_ln_g[l], cv_ln_b[l],
                             w_branch[l], w_out_mix[l])
        x = x + 0.5 * swiglu_ffn(rmsnorm(x, norm_ffn2[l]), ffn2_w_in[l], ffn2_w_out[l])
    return rmsnorm(x, norm_final)
```

```python
import functools

import jax
import jax.numpy as jnp
from jax import lax
from jax.experimental import pallas as pl
from jax.experimental.pallas import tpu as pltpu

F32 = jnp.float32
BF16 = jnp.bfloat16

HEAD_DIM = 128
HG_CHUNK = 64
LRU_C = 8.0
RMS_EPS = 1e-6
LN_EPS = 1e-5
V7X_VMEM_BYTES = 64 * 1024 * 1024
VMEM_LIMIT = V7X_VMEM_BYTES - 8 * 1024 * 1024


def _dot(a, b):
    return jnp.dot(a, b, preferred_element_type=F32)


def _dot_nt(a, b):
    return lax.dot_general(a, b, (((1,), (1,)), ((), ())), preferred_element_type=F32)


def _dot_tn(a, b):
    return lax.dot_general(a, b, (((0,), (0,)), ((), ())), preferred_element_type=F32)


def _sigmoid(x):
    return 1.0 / (1.0 + jnp.exp(-x))


def _silu(x):
    return x * _sigmoid(x)


def _rmsnorm(x, g):
    return x * lax.rsqrt(jnp.mean(x * x, axis=-1, keepdims=True) + RMS_EPS) * g


def _resident(shape):
    nd = len(shape)
    return pl.BlockSpec(shape, lambda *_: (0,) * nd, pipeline_mode=pl.Buffered(1))


def _ffn_kernel(x_ref, g_ref, win_ref, wout_ref, o_ref, *, d_ff, ff_chunk):
    x = x_ref[...]
    xn = _rmsnorm(x, g_ref[...]).astype(BF16)
    acc = jnp.zeros(x.shape, F32)
    for j in range(d_ff // ff_chunk):
        lo = j * ff_chunk
        gate = _dot(xn, win_ref[:, lo:lo + ff_chunk])
        up = _dot(xn, win_ref[:, d_ff + lo:d_ff + lo + ff_chunk])
        h = (_silu(gate) * up).astype(BF16)
        acc = acc + _dot(h, wout_ref[lo:lo + ff_chunk, :])
    o_ref[...] = x + 0.5 * acc


def _ffn(x2d, g, w_in, w_out, *, tm, ff_chunk):
    t, d = x2d.shape
    d_ff = w_out.shape[0]
    return pl.pallas_call(
        functools.partial(_ffn_kernel, d_ff=d_ff, ff_chunk=ff_chunk),
        out_shape=jax.ShapeDtypeStruct((t, d), F32),
        grid=(t // tm,),
        in_specs=[
            pl.BlockSpec((tm, d), lambda i: (i, 0)),
            _resident((1, d)),
            _resident((d, 2 * d_ff)),
            _resident((d_ff, d)),
        ],
        out_specs=pl.BlockSpec((tm, d), lambda i: (i, 0)),
        compiler_params=pltpu.CompilerParams(
            dimension_semantics=("parallel",), vmem_limit_bytes=VMEM_LIMIT),
        name="ffn",
    )(x2d, g, w_in, w_out)


def _split3(x):
    hi = x.astype(BF16)
    r = x - hi.astype(F32)
    mid = r.astype(BF16)
    lo = (r - mid.astype(F32)).astype(BF16)
    return hi, mid, lo


def _block_mid(b, hs):
    c = b.shape[0]
    blk = 2 * hs

    def rows_every8(off):
        return jnp.concatenate(
            [jnp.broadcast_to(b[r + off:r + off + 1, :], (8, b.shape[1])) for r in range(0, c, 8)], axis=0)

    if blk >= 8:
        return jnp.concatenate(
            [jnp.broadcast_to(b[r + hs - 1:r + hs, :], (blk, b.shape[1])) for r in range(0, c, blk)], axis=0)
    assert blk == 4
    row8 = lax.broadcasted_iota(jnp.int32, (c, 1), 0) % 8
    return jnp.where(row8 < 4, rows_every8(1), rows_every8(5))


def _hgrn2_chunk(q, k, v, f, b, st):
    c = q.shape[0]
    row = lax.broadcasted_iota(jnp.int32, (c, 1), 0)
    ri = lax.broadcasted_iota(jnp.int32, (c, c), 0)
    ci = lax.broadcasted_iota(jnp.int32, (c, c), 1)
    v16 = v.astype(BF16)

    a = jnp.zeros((c, c), F32)
    hs = c // 2
    while hs >= 1:
        blk = 2 * hs
        upper = (row % blk) >= hs
        if hs == 1:
            e = jnp.where(upper, f, 1.0)
        else:
            mid = _block_mid(b, hs)
            e = jnp.exp(jnp.where(upper, b - mid, mid - b))
        qe = jnp.where(upper, q * e, 0.0).astype(BF16)
        ke = jnp.where(upper, 0.0, k * e).astype(BF16)
        al = _dot_nt(qe, ke)
        if blk < c:
            al = jnp.where((ri // blk) == (ci // blk), al, 0.0)
        a = a + al
        hs //= 2

    o = _dot(a.astype(BF16), v16)
    o = o + jnp.sum(q * k, axis=-1, keepdims=True) * v
    o = o + _dot_nt((q * jnp.exp(b)).astype(BF16), st.astype(BF16))

    b_last = b[c - 1:c, :]
    kd = (k * jnp.exp(b_last - b)).astype(BF16)
    st_new = st * jnp.exp(b_last) + _dot_tn(v16, kd)
    return o, st_new


def _mixer_kernel(x_ref, nrm_ref, win_ref, bin_ref, lbl_ref, hgn_ref, lcw_ref, lcb_ref, lgw_ref,
                  lgb_ref, lam_ref, cw_ref, cb_ref, lng_ref, lnb_ref, wbr_ref, wout_ref,
                  o_ref,
                  st_ref, hl_ref, lxe_ref, cue_ref, q_ref, k_ref, v_ref, f_ref, b_ref, y_ref,
                  *, layer, n_heads):
    ts, d = x_ref.shape
    chunk = HG_CHUNK

    @pl.when(pl.program_id(1) == 0)
    def _():
        st_ref[...] = jnp.zeros(st_ref.shape, F32)
        hl_ref[...] = jnp.zeros(hl_ref.shape, F32)
        lxe_ref[0:8, :] = jnp.zeros((8, d), F32)
        cue_ref[0:32, :] = jnp.zeros((32, d), F32)

    x = x_ref[...]
    xn = _rmsnorm(x, nrm_ref[...]).astype(BF16)

    def proj(i):
        return _dot(xn, win_ref[:, i * d:(i + 1) * d]) + bin_ref[:, i * d:(i + 1) * d]

    lg = lbl_ref[...]
    ex = jnp.exp(lg - jnp.max(lg, axis=0, keepdims=True))
    sm = ex / jnp.sum(ex, axis=0, keepdims=True)
    lb = jnp.zeros((1, d), F32)
    for i in range(1, layer + 1):
        lb = lb + sm[i:i + 1, :]

    q_ref[...] = _silu(proj(0))
    z = proj(1)
    fgate = lb + (1.0 - lb) * _sigmoid(z)
    f_ref[...] = fgate
    k_ref[...] = (1.0 - lb) * _sigmoid(-z)
    v_ref[...] = proj(2)

    tri = (lax.broadcasted_iota(jnp.int32, (chunk, chunk), 0)
           >= lax.broadcasted_iota(jnp.int32, (chunk, chunk), 1)).astype(BF16)
    logf = jnp.log(fgate)
    for c in range(ts // chunk):
        g_hi, g_mid, g_lo = _split3(logf[c * chunk:(c + 1) * chunk, :])
        b_ref[c * chunk:(c + 1) * chunk, :] = _dot(tri, g_hi) + _dot(tri, g_mid) + _dot(tri, g_lo)

    hgn = hgn_ref[...]

    def chunk_body(c, carry):
        r0 = pl.multiple_of(c * chunk, chunk)
        rows = pl.ds(r0, chunk)
        for h in range(n_heads):
            cols = slice(h * HEAD_DIM, (h + 1) * HEAD_DIM)
            o, st_new = _hgrn2_chunk(q_ref[rows, cols], k_ref[rows, cols], v_ref[rows, cols],
                                     f_ref[rows, cols], b_ref[rows, cols], st_ref[h])
            st_ref[h] = st_new
            y_ref[rows, cols] = _rmsnorm(o, hgn)
        return carry

    lax.fori_loop(0, ts // chunk, chunk_body, 0)

    y_hg = (y_ref[...] * _silu(proj(3))).astype(BF16)
    merged = _sigmoid(proj(8)) * _dot(y_hg, wbr_ref[0])

    lxe_ref[8:8 + ts, :] = proj(4)
    xb = jnp.broadcast_to(lcb_ref[...], (ts, d))
    n_tap = lcw_ref.shape[0]
    for i in range(n_tap):
        xb = xb + lcw_ref[i:i + 1, :] * lxe_ref[8 - (n_tap - 1) + i:8 - (n_tap - 1) + i + ts, :]
    lxe_ref[0:8, :] = lxe_ref[ts:ts + 8, :]

    xb16 = xb.astype(BF16)
    gates = jnp.concatenate(
        [_dot(xb16[:, h * HEAD_DIM:(h + 1) * HEAD_DIM], lgw_ref[h]) for h in range(d // HEAD_DIM)], axis=-1)
    gr = jnp.concatenate([gates[:, h * 2 * HEAD_DIM:h * 2 * HEAD_DIM + HEAD_DIM]
                          for h in range(d // HEAD_DIM)], axis=-1) + lgb_ref[0:1, :]
    gi = jnp.concatenate([gates[:, h * 2 * HEAD_DIM + HEAD_DIM:(h + 1) * 2 * HEAD_DIM]
                          for h in range(d // HEAD_DIM)], axis=-1) + lgb_ref[1:2, :]
    lam = -lam_ref[...]
    softplus = jnp.maximum(lam, 0.0) + jnp.log(1.0 + jnp.exp(-jnp.abs(lam)))
    log_a = (-LRU_C) * _sigmoid(gr) * softplus
    a_t = jnp.exp(log_a)
    u_t = jnp.sqrt(1.0 - jnp.exp(2.0 * log_a)) * (_sigmoid(gi) * xb)

    trow = lax.broadcasted_iota(jnp.int32, (ts, 1), 0)
    aa, hh = a_t, u_t
    sh = 1
    while sh < ts:
        valid = trow >= sh
        h_prev = jnp.where(valid, pltpu.roll(hh, sh, 0), 0.0)
        a_prev = jnp.where(valid, pltpu.roll(aa, sh, 0), 1.0)
        hh = hh + aa * h_prev
        aa = aa * a_prev
        sh *= 2
    hh = hh + aa * hl_ref[...]
    hl_ref[...] = hh[ts - 1:ts, :]

    lgate = proj(5)
    gelu = 0.5 * lgate * (1.0 + jnp.tanh(0.7978845608028654 * (lgate + 0.044715 * (lgate * lgate * lgate))))
    merged = merged + _sigmoid(proj(9)) * _dot((hh * gelu).astype(BF16), wbr_ref[1])

    n_cv = cw_ref.shape[0]
    cue_ref[32:32 + ts, :] = proj(6) * _sigmoid(proj(7))
    cv = jnp.broadcast_to(cb_ref[...], (ts, d))
    for i in range(n_cv):
        off = 32 - (n_cv - 1) + i
        cv = cv + cw_ref[i:i + 1, :] * cue_ref[off:off + ts, :]
    cue_ref[0:32, :] = cue_ref[ts:ts + 32, :]

    mu = jnp.mean(cv, axis=-1, keepdims=True)
    xc = cv - mu
    ln = xc * lax.rsqrt(jnp.mean(xc * xc, axis=-1, keepdims=True) + LN_EPS) * lng_ref[...] + lnb_ref[...]
    merged = merged + _sigmoid(proj(10)) * _dot(_silu(ln).astype(BF16), wbr_ref[2])

    o_ref[...] = x + _dot(merged.astype(BF16), wout_ref[...])


def _mixer(x, layer, nrm, w_in, b_in, lb_logits, hg_norm, lcw, lcb, lgw, lgb, lam, cw, cb, lng, lnb,
           w_br, w_out, *, ts):
    b, s, d = x.shape
    n_heads = d // HEAD_DIM
    args = [x, nrm, w_in, b_in, lb_logits, hg_norm, lcw, lcb, lgw, lgb, lam, cw, cb, lng, lnb, w_br, w_out]
    in_specs = [pl.BlockSpec((None, ts, d), lambda bi, si: (bi, si, 0))]
    in_specs += [_resident(a.shape) for a in args[1:]]
    return pl.pallas_call(
        functools.partial(_mixer_kernel, layer=layer, n_heads=n_heads),
        out_shape=jax.ShapeDtypeStruct((b, s, d), F32),
        grid=(b, s // ts),
        in_specs=in_specs,
        out_specs=pl.BlockSpec((None, ts, d), lambda bi, si: (bi, si, 0)),
        scratch_shapes=[
            pltpu.VMEM((n_heads, HEAD_DIM, HEAD_DIM), F32),
            pltpu.VMEM((1, d), F32),
            pltpu.VMEM((ts + 8, d), F32),
            pltpu.VMEM((ts + 32, d), F32),
            pltpu.VMEM((ts, d), F32),
            pltpu.VMEM((ts, d), F32),
            pltpu.VMEM((ts, d), F32),
            pltpu.VMEM((ts, d), F32),
            pltpu.VMEM((ts, d), F32),
            pltpu.VMEM((ts, d), F32),
        ],
        compiler_params=pltpu.CompilerParams(
            dimension_semantics=("parallel", "arbitrary"), vmem_limit_bytes=VMEM_LIMIT),
        name="mixer",
    )(*args)


def _final_norm_kernel(x_ref, g_ref, o_ref):
    o_ref[...] = _rmsnorm(x_ref[...], g_ref[...])


def _final_norm(x2d, g, *, tm):
    t, d = x2d.shape
    return pl.pallas_call(
        _final_norm_kernel,
        out_shape=jax.ShapeDtypeStruct((t, d), F32),
        grid=(t // tm,),
        in_specs=[pl.BlockSpec((tm, d), lambda i: (i, 0)), _resident((1, d))],
        out_specs=pl.BlockSpec((tm, d), lambda i: (i, 0)),
        compiler_params=pltpu.CompilerParams(dimension_semantics=("parallel",)),
        name="final_norm",
    )(x2d, g)


def _tile(n, want):
    t = min(n, want)
    assert n % t == 0, (n, t)
    return t


def kernel(x, norm_ffn1, ffn1_w_in, ffn1_w_out, norm_mix, w_in_mix, b_in_mix, hgrn_lb_logits, hg_norm,
           lru_conv_w, lru_conv_b, lru_gate_w, lru_gate_b, lru_lambda, cv_dw_w, cv_dw_b, cv_ln_g, cv_ln_b,
           w_branch, w_out_mix, norm_ffn2, ffn2_w_in, ffn2_w_out, norm_final):
    bsz, seq, d = x.shape
    depth = norm_ffn1.shape[0]
    t = bsz * seq
    tm = _tile(t, 512)
    ts = _tile(seq, 256)
    assert ts % HG_CHUNK == 0 and d % HEAD_DIM == 0
    d_ff = ffn1_w_out.shape[1]
    ff_chunk = 256 if d_ff % 256 == 0 else d_ff

    row = lambda a: a.reshape(1, -1)
    for l in range(depth):
        x = _ffn(x.reshape(t, d), row(norm_ffn1[l]), ffn1_w_in[l].astype(BF16), ffn1_w_out[l].astype(BF16),
                 tm=tm, ff_chunk=ff_chunk).reshape(bsz, seq, d)
        lgw = jnp.concatenate([lru_gate_w[l, 0], lru_gate_w[l, 1]], axis=-1).astype(BF16)
        x = _mixer(x, l, row(norm_mix[l]), w_in_mix[l].astype(BF16), row(b_in_mix[l]), hgrn_lb_logits,
                   row(hg_norm[l]), lru_conv_w[l], row(lru_conv_b[l]), lgw, lru_gate_b[l], row(lru_lambda[l]),
                   cv_dw_w[l], row(cv_dw_b[l]), row(cv_ln_g[l]), row(cv_ln_b[l]),
                   w_branch[l].astype(BF16), w_out_mix[l].astype(BF16), ts=ts)
        x = _ffn(x.reshape(t, d), row(norm_ffn2[l]), ffn2_w_in[l].astype(BF16), ffn2_w_out[l].astype(BF16),
                 tm=tm, ff_chunk=ff_chunk).reshape(bsz, seq, d)
    return _final_norm(x.reshape(t, d), row(norm_final), tm=tm).reshape(bsz, seq, d)
```

```python
import functools

import jax
import jax.numpy as jnp
from jax import lax
from jax.experimental import pallas as pl
from jax.experimental.pallas import tpu as pltpu

F32 = jnp.float32
BF16 = jnp.bfloat16

SUBLANES = 8
HEAD_DIM = 128
HG_CHUNK = 64
LRU_C = 8.0
RMS_EPS = 1e-6
LN_EPS = 1e-5
V7X_VMEM_BYTES = 64 * 1024 * 1024
VMEM_LIMIT = V7X_VMEM_BYTES - 8 * 1024 * 1024


def _dot(a, b):
    return jnp.dot(a, b, preferred_element_type=F32)


def _dot_nt(a, b):
    return lax.dot_general(a, b, (((1,), (1,)), ((), ())), preferred_element_type=F32)


def _dot_tn(a, b):
    return lax.dot_general(a, b, (((0,), (0,)), ((), ())), preferred_element_type=F32)


def _sigmoid(x):
    return 1.0 / (1.0 + jnp.exp(-x))


def _silu(x):
    return x * _sigmoid(x)


def _rmsnorm(x, g):
    return x * lax.rsqrt(jnp.mean(x * x, axis=-1, keepdims=True) + RMS_EPS) * g


def _resident(shape):
    nd = len(shape)
    return pl.BlockSpec(shape, lambda *_: (0,) * nd, pipeline_mode=pl.Buffered(1))


def _layer_resident(shape, layer):
    nd = len(shape)
    return pl.BlockSpec((None,) + tuple(shape[1:]), lambda *_: (layer,) + (0,) * (nd - 1),
                        pipeline_mode=pl.Buffered(1))


def _ffn_kernel(x_ref, g_ref, win_ref, wout_ref, o_ref, *, d_ff, ff_chunk):
    x = x_ref[...]
    xn = _rmsnorm(x, g_ref[...]).astype(BF16)
    acc = jnp.zeros(x.shape, F32)
    for j in range(d_ff // ff_chunk):
        lo = j * ff_chunk
        gate = _dot(xn, win_ref[:, lo:lo + ff_chunk])
        up = _dot(xn, win_ref[:, d_ff + lo:d_ff + lo + ff_chunk])
        h = (_silu(gate) * up).astype(BF16)
        acc = acc + _dot(h, wout_ref[lo:lo + ff_chunk, :])
    o_ref[...] = x + 0.5 * acc


def _ffn(x2d, layer, g, w_in, w_out, *, tm, ff_chunk):
    t, d = x2d.shape
    d_ff = w_out.shape[1]
    return pl.pallas_call(
        functools.partial(_ffn_kernel, d_ff=d_ff, ff_chunk=ff_chunk),
        out_shape=jax.ShapeDtypeStruct((t, d), F32),
        grid=(t // tm,),
        in_specs=[
            pl.BlockSpec((tm, d), lambda i: (i, 0)),
            _layer_resident(g.shape, layer),
            _layer_resident(w_in.shape, layer),
            _layer_resident(w_out.shape, layer),
        ],
        out_specs=pl.BlockSpec((tm, d), lambda i: (i, 0)),
        compiler_params=pltpu.CompilerParams(
            dimension_semantics=("parallel",), vmem_limit_bytes=VMEM_LIMIT),
        name="ffn",
    )(x2d, g, w_in, w_out)


def _split3(x):
    hi = x.astype(BF16)
    r = x - hi.astype(F32)
    mid = r.astype(BF16)
    lo = (r - mid.astype(F32)).astype(BF16)
    return hi, mid, lo


def _block_mid(b, hs):
    c = b.shape[0]
    blk = 2 * hs

    def rows_every8(off):
        return jnp.concatenate(
            [jnp.broadcast_to(b[r + off:r + off + 1, :], (SUBLANES, b.shape[1]))
             for r in range(0, c, SUBLANES)], axis=0)

    if blk >= SUBLANES:
        return jnp.concatenate(
            [jnp.broadcast_to(b[r + hs - 1:r + hs, :], (blk, b.shape[1])) for r in range(0, c, blk)], axis=0)
    assert blk == 4
    row8 = lax.broadcasted_iota(jnp.int32, b.shape, 0) % SUBLANES
    return jnp.where(row8 < 4, rows_every8(1), rows_every8(5))


def _level_masks(c):
    ri = lax.broadcasted_iota(jnp.int32, (c, c), 0)
    ci = lax.broadcasted_iota(jnp.int32, (c, c), 1)
    masks = []
    hs = c // 2
    while hs >= 1:
        blk = 2 * hs
        masks.append(((ri // blk) == (ci // blk)) & ((ri % blk) >= hs) & ((ci % blk) < hs))
        hs //= 2
    return masks


def _hgrn2_scores(q, k, f, b):
    c = q.shape[0]
    odd = (lax.broadcasted_iota(jnp.int32, q.shape, 0) % 2) == 1
    out = []
    hs = c // 2
    while hs >= 1:
        e = jnp.where(odd, f, 1.0) if hs == 1 else jnp.exp(-jnp.abs(b - _block_mid(b, hs)))
        out.append(_dot_nt((q * e).astype(BF16), (k * e).astype(BF16)))
        hs //= 2
    return out


def _causal_conv(ext_ref, hist, w_ref, bias_ref, out_ref, *, row_block, lane_block):
    n_tap = w_ref.shape[0]
    ts, d = out_ref.shape
    base = hist - (n_tap - 1)
    for c0 in range(0, ts, row_block):
        for l0 in range(0, d, lane_block):
            lanes = slice(l0, l0 + lane_block)
            acc = jnp.broadcast_to(bias_ref[:, lanes], (row_block, lane_block))
            for r in range(SUBLANES):
                taps = [i for i in range(n_tap) if (base + i) % SUBLANES == r]
                if not taps:
                    continue
                rows = row_block if r == 0 else row_block + SUBLANES
                p = None
                for i in taps:
                    a0 = c0 + base + i - r
                    term = w_ref[i:i + 1, lanes] * ext_ref[a0:a0 + rows, lanes]
                    p = term if p is None else p + term
                acc = acc + (p if r == 0 else p[r:r + row_block, :])
            out_ref[c0:c0 + row_block, lanes] = acc


def _lru_scan(a_t, u_t, carry_ref, out_ref):
    ts, d = a_t.shape
    groups = ts // SUBLANES
    a3 = a_t.reshape(groups, SUBLANES, d)
    h3 = u_t.reshape(groups, SUBLANES, d)
    row = lax.broadcasted_iota(jnp.int32, a3.shape, 1)
    sh = 1
    while sh < SUBLANES:
        valid = row >= sh
        h_prev = jnp.where(valid, pltpu.roll(h3, sh, 1), 0.0)
        a_prev = jnp.where(valid, pltpu.roll(a3, sh, 1), 1.0)
        h3 = h3 + a3 * h_prev
        a3 = a3 * a_prev
        sh *= 2
    carry = carry_ref[...]
    for g in range(groups):
        hg = h3[g] + a3[g] * carry
        out_ref[g * SUBLANES:(g + 1) * SUBLANES, :] = hg
        carry = hg[SUBLANES - 1:SUBLANES, :]
    carry_ref[...] = carry


def _mixer_kernel(x_ref, nrm_ref, win_ref, bin_ref, lbl_ref, hgn_ref, lcw_ref, lcb_ref, lgw_ref,
                  lgb_ref, lam_ref, cw_ref, cb_ref, lng_ref, lnb_ref, wbr_ref, wout_ref,
                  o_ref,
                  st_ref, hl_ref, lxe_ref, cue_ref, xn_ref, q_ref, f_ref, v_ref, b_ref, y_ref,
                  *, layer, n_heads):
    ts, d = x_ref.shape
    chunk = HG_CHUNK
    lru_hist, cv_hist = SUBLANES, 4 * SUBLANES

    @pl.when(pl.program_id(1) == 0)
    def _():
        st_ref[...] = jnp.zeros(st_ref.shape, F32)
        hl_ref[...] = jnp.zeros(hl_ref.shape, F32)
        lxe_ref[0:lru_hist, :] = jnp.zeros((lru_hist, d), F32)
        cue_ref[0:cv_hist, :] = jnp.zeros((cv_hist, d), F32)

    x = x_ref[...]
    xn_ref[...] = _rmsnorm(x, nrm_ref[...]).astype(BF16)

    def proj(i):
        return _dot(xn_ref[...], win_ref[:, i * d:(i + 1) * d]) + bin_ref[:, i * d:(i + 1) * d]

    lg = lbl_ref[...]
    ex = jnp.exp(lg - jnp.max(lg, axis=0, keepdims=True))
    sm = ex / jnp.sum(ex, axis=0, keepdims=True)
    lb = jnp.zeros((1, d), F32)
    for i in range(1, layer + 1):
        lb = lb + sm[i:i + 1, :]

    q_ref[...] = _silu(proj(0))
    fgate = lb + (1.0 - lb) * _sigmoid(proj(1))
    f_ref[...] = fgate
    v_ref[...] = proj(2)

    tri = (lax.broadcasted_iota(jnp.int32, (chunk, chunk), 0)
           >= lax.broadcasted_iota(jnp.int32, (chunk, chunk), 1)).astype(BF16)
    logf = jnp.log(fgate)
    for c in range(ts // chunk):
        g_hi, g_mid, g_lo = _split3(logf[c * chunk:(c + 1) * chunk, :])
        b_ref[c * chunk:(c + 1) * chunk, :] = _dot(tri, g_hi) + _dot(tri, g_mid) + _dot(tri, g_lo)

    hgn = hgn_ref[...]

    def chunk_body(c, carry):
        rows = pl.ds(pl.multiple_of(c * chunk, chunk), chunk)
        masks = _level_masks(chunk)
        heads = [slice(h * HEAD_DIM, (h + 1) * HEAD_DIM) for h in range(n_heads)]
        scores, carried = [], []
        for h, cols in enumerate(heads):
            q, f, v, b = q_ref[rows, cols], f_ref[rows, cols], v_ref[rows, cols], b_ref[rows, cols]
            k = 1.0 - f
            scores.append(_hgrn2_scores(q, k, f, b))
            st = st_ref[h]
            carried.append(_dot_nt((q * jnp.exp(b)).astype(BF16), st.astype(BF16)))
            b_last = b[chunk - 1:chunk, :]
            kd = (k * jnp.exp(b_last - b)).astype(BF16)
            st_ref[h] = st * jnp.exp(b_last) + _dot_tn(v.astype(BF16), kd)
        for h, cols in enumerate(heads):
            q, f, v = q_ref[rows, cols], f_ref[rows, cols], v_ref[rows, cols]
            a = jnp.zeros((chunk, chunk), F32)
            for m, al in zip(masks, scores[h]):
                a = jnp.where(m, al, a)
            o = _dot(a.astype(BF16), v.astype(BF16)) + carried[h]
            o = o + jnp.sum(q * (1.0 - f), axis=-1, keepdims=True) * v
            y_ref[rows, cols] = _rmsnorm(o, hgn)
        return carry

    lax.fori_loop(0, ts // chunk, chunk_body, 0)

    y_hg = (y_ref[...] * _silu(proj(3))).astype(BF16)
    merged = _sigmoid(proj(8)) * _dot(y_hg, wbr_ref[0])

    lxe_ref[lru_hist:lru_hist + ts, :] = proj(4)
    xb_ref = q_ref
    _causal_conv(lxe_ref, lru_hist, lcw_ref, lcb_ref, xb_ref, row_block=64, lane_block=256)
    lxe_ref[0:lru_hist, :] = lxe_ref[ts:ts + lru_hist, :]

    xb = xb_ref[...]
    xb16 = xb.astype(BF16)
    gates = [_dot(xb16[:, h * HEAD_DIM:(h + 1) * HEAD_DIM], lgw_ref[h]) for h in range(d // HEAD_DIM)]
    gr = jnp.concatenate([g[:, :HEAD_DIM] for g in gates], axis=-1) + lgb_ref[0:1, :]
    gi = jnp.concatenate([g[:, HEAD_DIM:] for g in gates], axis=-1) + lgb_ref[1:2, :]
    lam = -lam_ref[...]
    softplus = jnp.maximum(lam, 0.0) + jnp.log(1.0 + jnp.exp(-jnp.abs(lam)))
    log_a = (-LRU_C) * _sigmoid(gr) * softplus
    a_t = jnp.exp(log_a)
    u_t = jnp.sqrt(1.0 - a_t * a_t) * (_sigmoid(gi) * xb)
    h_ref = f_ref
    _lru_scan(a_t, u_t, hl_ref, h_ref)

    lgate = proj(5)
    gelu = 0.5 * lgate * (1.0 + jnp.tanh(0.7978845608028654 * (lgate + 0.044715 * (lgate * lgate * lgate))))
    merged = merged + _sigmoid(proj(9)) * _dot((h_ref[...] * gelu).astype(BF16), wbr_ref[1])

    cue_ref[cv_hist:cv_hist + ts, :] = proj(6) * _sigmoid(proj(7))
    cv_ref = v_ref
    _causal_conv(cue_ref, cv_hist, cw_ref, cb_ref, cv_ref, row_block=64, lane_block=256)
    cue_ref[0:cv_hist, :] = cue_ref[ts:ts + cv_hist, :]

    cv = cv_ref[...]
    mu = jnp.mean(cv, axis=-1, keepdims=True)
    xc = cv - mu
    ln = xc * lax.rsqrt(jnp.mean(xc * xc, axis=-1, keepdims=True) + LN_EPS) * lng_ref[...] + lnb_ref[...]
    merged = merged + _sigmoid(proj(10)) * _dot(_silu(ln).astype(BF16), wbr_ref[2])

    o_ref[...] = x + _dot(merged.astype(BF16), wout_ref[...])


def _mixer(x, layer, lb_logits, stacked, *, ts):
    b, s, d = x.shape
    n_heads = d // HEAD_DIM
    nrm, w_in, b_in = stacked[:3]
    args = [x, nrm, w_in, b_in, lb_logits] + list(stacked[3:])
    in_specs = [pl.BlockSpec((None, ts, d), lambda bi, si: (bi, si, 0))]
    in_specs += [_layer_resident(a.shape, layer) for a in (nrm, w_in, b_in)]
    in_specs += [_resident(lb_logits.shape)]
    in_specs += [_layer_resident(a.shape, layer) for a in stacked[3:]]
    return pl.pallas_call(
        functools.partial(_mixer_kernel, layer=layer, n_heads=n_heads),
        out_shape=jax.ShapeDtypeStruct((b, s, d), F32),
        grid=(b, s // ts),
        in_specs=in_specs,
        out_specs=pl.BlockSpec((None, ts, d), lambda bi, si: (bi, si, 0)),
        scratch_shapes=[
            pltpu.VMEM((n_heads, HEAD_DIM, HEAD_DIM), F32),
            pltpu.VMEM((1, d), F32),
            pltpu.VMEM((ts + SUBLANES, d), F32),
            pltpu.VMEM((ts + 4 * SUBLANES, d), F32),
            pltpu.VMEM((ts, d), BF16),
            pltpu.VMEM((ts, d), F32),
            pltpu.VMEM((ts, d), F32),
            pltpu.VMEM((ts, d), F32),
            pltpu.VMEM((ts, d), F32),
            pltpu.VMEM((ts, d), F32),
        ],
        compiler_params=pltpu.CompilerParams(
            dimension_semantics=("parallel", "arbitrary"), vmem_limit_bytes=VMEM_LIMIT),
        name="mixer",
    )(*args)


def _final_norm_kernel(x_ref, g_ref, o_ref):
    o_ref[...] = _rmsnorm(x_ref[...], g_ref[...])


def _final_norm(x2d, g, *, tm):
    t, d = x2d.shape
    return pl.pallas_call(
        _final_norm_kernel,
        out_shape=jax.ShapeDtypeStruct((t, d), F32),
        grid=(t // tm,),
        in_specs=[pl.BlockSpec((tm, d), lambda i: (i, 0)), _resident((1, d))],
        out_specs=pl.BlockSpec((tm, d), lambda i: (i, 0)),
        compiler_params=pltpu.CompilerParams(dimension_semantics=("parallel",)),
        name="final_norm",
    )(x2d, g)


def _tile(n, want):
    t = min(n, want)
    assert n % t == 0, (n, t)
    return t


def kernel(x, norm_ffn1, ffn1_w_in, ffn1_w_out, norm_mix, w_in_mix, b_in_mix, hgrn_lb_logits, hg_norm,
           lru_conv_w, lru_conv_b, lru_gate_w, lru_gate_b, lru_lambda, cv_dw_w, cv_dw_b, cv_ln_g, cv_ln_b,
           w_branch, w_out_mix, norm_ffn2, ffn2_w_in, ffn2_w_out, norm_final):
    bsz, seq, d = x.shape
    depth = norm_ffn1.shape[0]
    t = bsz * seq
    tm = _tile(t, 512)
    ts = _tile(seq, 256)
    assert ts % HG_CHUNK == 0 and d % HEAD_DIM == 0
    d_ff = ffn1_w_out.shape[1]
    ff_chunk = 256 if d_ff % 256 == 0 else d_ff

    rows = lambda a: a[:, None, :]
    ffn1 = (rows(norm_ffn1), ffn1_w_in.astype(BF16), ffn1_w_out.astype(BF16))
    ffn2 = (rows(norm_ffn2), ffn2_w_in.astype(BF16), ffn2_w_out.astype(BF16))
    lgw = jnp.concatenate([lru_gate_w[:, 0], lru_gate_w[:, 1]], axis=-1).astype(BF16)
    mix = (rows(norm_mix), w_in_mix.astype(BF16), rows(b_in_mix), rows(hg_norm), lru_conv_w, rows(lru_conv_b),
           lgw, lru_gate_b, rows(lru_lambda), cv_dw_w, rows(cv_dw_b), rows(cv_ln_g), rows(cv_ln_b),
           w_branch.astype(BF16), w_out_mix.astype(BF16))

    for l in range(depth):
        x = _ffn(x.reshape(t, d), l, *ffn1, tm=tm, ff_chunk=ff_chunk).reshape(bsz, seq, d)
        x = _mixer(x, l, hgrn_lb_logits, mix, ts=ts)
        x = _ffn(x.reshape(t, d), l, *ffn2, tm=tm, ff_chunk=ff_chunk).reshape(bsz, seq, d)
    return _final_norm(x.reshape(t, d), norm_final.reshape(1, d), tm=tm).reshape(bsz, seq, d)
```

```python
import functools

import jax
import jax.numpy as jnp
from jax import lax
from jax.experimental import pallas as pl
from jax.experimental.pallas import tpu as pltpu

F32 = jnp.float32
BF16 = jnp.bfloat16

SUBLANES = 8
LANES = 128
MXU_WIDTH = 256
HEAD_DIM = 128
HG_CHUNK = 64
LRU_C = 8.0
RMS_EPS = 1e-6
LN_EPS = 1e-5
V7X_VMEM_BYTES = 64 * 1024 * 1024
VMEM_LIMIT = V7X_VMEM_BYTES - 8 * 1024 * 1024


def _dot(a, b):
    return jnp.dot(a, b, preferred_element_type=F32)


def _dot_nt(a, b):
    return lax.dot_general(a, b, (((1,), (1,)), ((), ())), preferred_element_type=F32)


def _dot_tn(a, b):
    return lax.dot_general(a, b, (((0,), (0,)), ((), ())), preferred_element_type=F32)


def _panel(w_refs, c):
    width = w_refs[0].shape[1]
    assert width % MXU_WIDTH == 0
    return w_refs[c // width][:, c % width:c % width + MXU_WIDTH]


def _split_cols(w, n_parts):
    width = w.shape[-1] // n_parts
    assert width * n_parts == w.shape[-1] and width % MXU_WIDTH == 0 and width % 1024 != 0
    return [w[..., i * width:(i + 1) * width] for i in range(n_parts)]


def _sigmoid(x):
    return 1.0 / (1.0 + jnp.exp(-x))


def _silu(x):
    return x * _sigmoid(x)


def _gelu_tanh(x):
    return 0.5 * x * (1.0 + jnp.tanh(0.7978845608028654 * (x + 0.044715 * (x * x * x))))


def _rmsnorm(x, g):
    return x * lax.rsqrt(jnp.mean(x * x, axis=-1, keepdims=True) + RMS_EPS) * g


def _resident(shape):
    nd = len(shape)
    return pl.BlockSpec(shape, lambda *_: (0,) * nd, pipeline_mode=pl.Buffered(1))


def _layer_resident(shape, layer):
    nd = len(shape)
    return pl.BlockSpec((None,) + tuple(shape[1:]), lambda *_: (layer,) + (0,) * (nd - 1),
                        pipeline_mode=pl.Buffered(1))


def _emit_interleaved(vector_tasks, matmul_tasks):
    n, m = len(vector_tasks), len(matmul_tasks)
    i = j = 0
    while i < n or j < m:
        if j >= m or (i < n and i * m <= j * n):
            vector_tasks[i]()
            i += 1
        else:
            matmul_tasks[j]()
            j += 1


def _ffn_kernel(x_ref, g_ref, win_ref, wout_ref, o_ref, *, d_ff, ff_chunk):
    x = x_ref[...]
    xn = _rmsnorm(x, g_ref[...]).astype(BF16)
    acc = jnp.zeros(x.shape, F32)
    for j in range(d_ff // ff_chunk):
        lo = j * ff_chunk
        gate = _dot(xn, win_ref[:, lo:lo + ff_chunk])
        up = _dot(xn, win_ref[:, d_ff + lo:d_ff + lo + ff_chunk])
        h = (_silu(gate) * up).astype(BF16)
        acc = acc + _dot(h, wout_ref[lo:lo + ff_chunk, :])
    o_ref[...] = x + 0.5 * acc


def _ffn(x2d, layer, g, w_in, w_out, *, tm, ff_chunk):
    t, d = x2d.shape
    d_ff = w_out.shape[1]
    return pl.pallas_call(
        functools.partial(_ffn_kernel, d_ff=d_ff, ff_chunk=ff_chunk),
        out_shape=jax.ShapeDtypeStruct((t, d), F32),
        grid=(t // tm,),
        in_specs=[
            pl.BlockSpec((tm, d), lambda i: (i, 0)),
            _layer_resident(g.shape, layer),
            _layer_resident(w_in.shape, layer),
            _layer_resident(w_out.shape, layer),
        ],
        out_specs=pl.BlockSpec((tm, d), lambda i: (i, 0)),
        compiler_params=pltpu.CompilerParams(
            dimension_semantics=("parallel",), vmem_limit_bytes=VMEM_LIMIT),
        name="ffn",
    )(x2d, g, w_in, w_out)


def _split3(x):
    hi = x.astype(BF16)
    r = x - hi.astype(F32)
    mid = r.astype(BF16)
    lo = (r - mid.astype(F32)).astype(BF16)
    return hi, mid, lo


def _block_mid(b, hs):
    c = b.shape[0]
    blk = 2 * hs

    def rows_every8(off):
        return jnp.concatenate(
            [jnp.broadcast_to(b[r + off:r + off + 1, :], (SUBLANES, b.shape[1]))
             for r in range(0, c, SUBLANES)], axis=0)

    if blk >= SUBLANES:
        return jnp.concatenate(
            [jnp.broadcast_to(b[r + hs - 1:r + hs, :], (blk, b.shape[1])) for r in range(0, c, blk)], axis=0)
    assert blk == 4
    row8 = lax.broadcasted_iota(jnp.int32, b.shape, 0) % SUBLANES
    return jnp.where(row8 < 4, rows_every8(1), rows_every8(5))


def _level_masks(c):
    ri = lax.broadcasted_iota(jnp.int32, (c, c), 0)
    ci = lax.broadcasted_iota(jnp.int32, (c, c), 1)
    masks = []
    hs = c // 2
    while hs >= 1:
        blk = 2 * hs
        masks.append(((ri // blk) == (ci // blk)) & ((ri % blk) >= hs) & ((ci % blk) < hs))
        hs //= 2
    return masks


def _hgrn2_scores(q, k, f, b):
    c = q.shape[0]
    odd = (lax.broadcasted_iota(jnp.int32, q.shape, 0) % 2) == 1
    out = []
    hs = c // 2
    while hs >= 1:
        e = jnp.where(odd, f, 1.0) if hs == 1 else jnp.exp2(-jnp.abs(b - _block_mid(b, hs)))
        out.append(_dot_nt((q * e).astype(BF16), (k * e).astype(BF16)))
        hs //= 2
    return out


def _conv_unit(ext_ref, hist, w_ref, bias_ref, out_ref, *, row_block, lane_block):
    n_tap = w_ref.shape[0]
    base = hist - (n_tap - 1)
    groups = row_block // SUBLANES

    def unit(c0, lanes):
        sub = lax.broadcasted_iota(jnp.int32, (groups, SUBLANES, lane_block), 1)
        acc = jnp.broadcast_to(bias_ref[:, lanes][None], (groups, SUBLANES, lane_block))
        for r in range(SUBLANES):
            taps = [i for i in range(n_tap) if (base + i) % SUBLANES == r]
            if not taps:
                continue
            rows = row_block if r == 0 else row_block + SUBLANES
            p = None
            for i in taps:
                a0 = c0 + base + i - r
                win = ext_ref[a0:a0 + rows, lanes].reshape(rows // SUBLANES, SUBLANES, lane_block)
                term = win * w_ref[i, :, lanes][None]
                p = term if p is None else p + term
            if r:
                pr = pltpu.roll(p, SUBLANES - r, 1)
                p = jnp.where(sub < SUBLANES - r, pr[0:groups], pr[1:groups + 1])
            acc = acc + p
        out_ref[c0:c0 + row_block, lanes] = acc.reshape(row_block, lane_block)

    return unit


def _lru_scan(a_t, u_t, carry):
    ts, n = a_t.shape
    groups = ts // SUBLANES
    a3 = a_t.reshape(groups, SUBLANES, n)
    h3 = u_t.reshape(groups, SUBLANES, n)
    row = lax.broadcasted_iota(jnp.int32, a3.shape, 1)
    sh = 1
    while sh < SUBLANES:
        valid = row >= sh
        h_prev = jnp.where(valid, pltpu.roll(h3, sh, 1), 0.0)
        a_prev = jnp.where(valid, pltpu.roll(a3, sh, 1), 1.0)
        h3 = h3 + a3 * h_prev
        a3 = a3 * a_prev
        sh *= 2
    out = []
    for g in range(groups):
        hg = h3[g] + a3[g] * carry
        out.append(hg)
        carry = hg[SUBLANES - 1:SUBLANES, :]
    return jnp.concatenate(out, axis=0), carry


def _mixer_kernel(x_ref, nrm_ref, win0_ref, win1_ref, bin_ref, lbl_ref, hgn_ref, lcw_ref, lcb_ref, lgw_ref,
                  lgb_ref, lam_ref, cw_ref, cb_ref, lng_ref, lnb_ref, wbr0_ref, wbr1_ref, wout0_ref, wout1_ref,
                  o_ref,
                  st_ref, hl_ref, lxe_ref, cue_ref, xn_ref, y16_ref, a16_ref,
                  q_ref, f_ref, v_ref, b_ref, y_ref, m_ref, cv_ref,
                  *, layer, n_heads):
    ts, d = x_ref.shape
    chunk = HG_CHUNK
    lru_hist, cv_hist = SUBLANES, 4 * SUBLANES
    panels = [slice(c, c + MXU_WIDTH) for c in range(0, d, MXU_WIDTH)]
    win_refs, wbr_refs, wout_refs = (win0_ref, win1_ref), (wbr0_ref, wbr1_ref), (wout0_ref, wout1_ref)

    @pl.when(pl.program_id(1) == 0)
    def _():
        st_ref[...] = jnp.zeros(st_ref.shape, F32)
        hl_ref[...] = jnp.zeros(hl_ref.shape, F32)
        lxe_ref[0:lru_hist, :] = jnp.zeros((lru_hist, d), F32)
        cue_ref[0:cv_hist, :] = jnp.zeros((cv_hist, d), F32)

    xn_ref[...] = _rmsnorm(x_ref[...], nrm_ref[...]).astype(BF16)

    def proj(i, p):
        c = i * d + p.start
        return _dot(xn_ref[...], _panel(win_refs, c)) + bin_ref[:, c:c + MXU_WIDTH]

    def branch_proj(a16_ref_, i, p):
        return _dot(a16_ref_[...], _panel(wbr_refs, i * d + p.start))

    lg = lbl_ref[...]
    ex = jnp.exp(lg - jnp.max(lg, axis=0, keepdims=True))
    sm = ex / jnp.sum(ex, axis=0, keepdims=True)
    lb = jnp.zeros((1, d), F32)
    for i in range(1, layer + 1):
        lb = lb + sm[i:i + 1, :]

    for p in panels:
        lxe_ref[lru_hist:lru_hist + ts, p] = proj(4, p)
    for p in panels:
        cue_ref[cv_hist:cv_hist + ts, p] = proj(6, p) * _sigmoid(proj(7, p))
    xb_ref = y_ref

    def q_task(p):
        q_ref[:, p] = _silu(proj(0, p))

    def f_task(p):
        f_ref[:, p] = lb[:, p] + (1.0 - lb[:, p]) * _sigmoid(proj(1, p))

    def v_task(p):
        v_ref[:, p] = proj(2, p)

    def lgate_task(p):
        m_ref[:, p] = _gelu_tanh(proj(5, p))

    conv_rows = ts // 2
    lru_conv = _conv_unit(lxe_ref, lru_hist, lcw_ref, lcb_ref, xb_ref, row_block=conv_rows, lane_block=LANES)
    cv_conv = _conv_unit(cue_ref, cv_hist, cw_ref, cb_ref, cv_ref, row_block=conv_rows, lane_block=LANES)

    def conv_task(l0):
        lanes = slice(l0, l0 + LANES)
        cv_conv(0, lanes)
        lru_conv(0, lanes)
        lru_conv(conv_rows, lanes)

    _emit_interleaved(
        [functools.partial(conv_task, l0) for l0 in range(0, d, LANES)],
        [functools.partial(t, p) for t in (q_task, f_task, v_task, lgate_task) for p in panels])
    lxe_ref[0:lru_hist, :] = lxe_ref[ts:ts + lru_hist, :]

    tri = (lax.broadcasted_iota(jnp.int32, (chunk, chunk), 0)
           >= lax.broadcasted_iota(jnp.int32, (chunk, chunk), 1)).astype(BF16)
    for c in range(ts // chunk):
        rows = slice(c * chunk, (c + 1) * chunk)
        g_hi, g_mid, g_lo = _split3(jnp.log2(f_ref[rows, :]))
        b_ref[rows, :] = _dot(tri, g_hi) + _dot(tri, g_mid) + _dot(tri, g_lo)

    lam = -lam_ref[...]
    softplus = jnp.maximum(lam, 0.0) + jnp.log(1.0 + jnp.exp(-jnp.abs(lam)))

    def lru_task(h):
        cols = slice(h * HEAD_DIM, (h + 1) * HEAD_DIM)
        xb = xb_ref[:, cols]
        gates = _dot(xb.astype(BF16), lgw_ref[h])
        gr = gates[:, :HEAD_DIM] + lgb_ref[0:1, cols]
        gi = gates[:, HEAD_DIM:] + lgb_ref[1:2, cols]
        a_t = jnp.exp((-LRU_C) * _sigmoid(gr) * softplus[:, cols])
        u_t = jnp.sqrt(1.0 - a_t * a_t) * (_sigmoid(gi) * xb)
        hh, last = _lru_scan(a_t, u_t, hl_ref[:, cols])
        hl_ref[:, cols] = last
        a16_ref[:, cols] = (hh * m_ref[:, cols]).astype(BF16)

    for h in range(d // HEAD_DIM):
        lru_task(h)

    hgn = hgn_ref[...]

    def chunk_body(c, carry):
        rows = pl.ds(pl.multiple_of(c * chunk, chunk), chunk)
        masks = _level_masks(chunk)
        heads = [slice(h * HEAD_DIM, (h + 1) * HEAD_DIM) for h in range(n_heads)]
        scores, carried = [], []
        for h, cols in enumerate(heads):
            q, f, v, b = q_ref[rows, cols], f_ref[rows, cols], v_ref[rows, cols], b_ref[rows, cols]
            k = 1.0 - f
            scores.append(_hgrn2_scores(q, k, f, b))
            st = st_ref[h]
            carried.append(_dot_nt((q * jnp.exp2(b)).astype(BF16), st.astype(BF16)))
            b_last = b[chunk - 1:chunk, :]
            kd = (k * jnp.exp2(b_last - b)).astype(BF16)
            st_ref[h] = st * jnp.exp2(b_last) + _dot_tn(v.astype(BF16), kd)
        for h, cols in enumerate(heads):
            q, f, v = q_ref[rows, cols], f_ref[rows, cols], v_ref[rows, cols]
            a = jnp.zeros((chunk, chunk), F32)
            for m, al in zip(masks, scores[h]):
                a = jnp.where(m, al, a)
            o = _dot(a.astype(BF16), v.astype(BF16)) + carried[h]
            o = o + jnp.sum(q * (1.0 - f), axis=-1, keepdims=True) * v
            y_ref[rows, cols] = _rmsnorm(o, hgn)
        return carry

    lax.fori_loop(0, ts // chunk, chunk_body, 0)

    ga_ref, gb_ref, gc_ref = q_ref, f_ref, v_ref
    half_width = win0_ref.shape[1]

    def panel_body(j, carry):
        c = pl.multiple_of(j * MXU_WIDTH, MXU_WIDTH)
        p = pl.ds(c, MXU_WIDTH)

        def proj_dyn(i):
            half, off = divmod(i * d, half_width)
            assert off + d <= half_width
            w = win_refs[half][:, pl.ds(pl.multiple_of(off + c, MXU_WIDTH), MXU_WIDTH)]
            return _dot(xn_ref[...], w) + bin_ref[:, pl.ds(pl.multiple_of(i * d + c, MXU_WIDTH), MXU_WIDTH)]

        conv_units = [functools.partial(cv_conv, conv_rows, pl.ds(pl.multiple_of(c + dl, LANES), LANES))
                      for dl in range(0, MXU_WIDTH, LANES)]

        def yhg_task():
            y16_ref[:, p] = (y_ref[:, p] * _silu(proj_dyn(3))).astype(BF16)

        def gate_task(ref, i):
            ref[:, p] = _sigmoid(proj_dyn(i))

        _emit_interleaved(conv_units, [yhg_task, functools.partial(gate_task, ga_ref, 8),
                                       functools.partial(gate_task, gb_ref, 9),
                                       functools.partial(gate_task, gc_ref, 10)])
        return carry

    lax.fori_loop(0, len(panels), panel_body, 0)
    cue_ref[0:cv_hist, :] = cue_ref[ts:ts + cv_hist, :]

    def ln_task(r0):
        rows = slice(r0, r0 + conv_rows)
        cv = cv_ref[rows, :]
        mu = jnp.mean(cv, axis=-1, keepdims=True)
        xc = cv - mu
        ln = xc * lax.rsqrt(jnp.mean(xc * xc, axis=-1, keepdims=True) + LN_EPS) * lng_ref[...] + lnb_ref[...]
        xn_ref[rows, :] = _silu(ln).astype(BF16)

    def merge_ab_task(p):
        m_ref[:, p] = (ga_ref[:, p] * branch_proj(y16_ref, 0, p)
                       + gb_ref[:, p] * branch_proj(a16_ref, 1, p))

    _emit_interleaved([functools.partial(ln_task, r0) for r0 in range(0, ts, conv_rows)],
                      [functools.partial(merge_ab_task, p) for p in panels])
    for p in panels:
        y16_ref[:, p] = (m_ref[:, p] + gc_ref[:, p] * branch_proj(xn_ref, 2, p)).astype(BF16)
    for p in panels:
        o_ref[:, p] = x_ref[:, p] + _dot(y16_ref[...], _panel(wout_refs, p.start))


def _mixer(x, layer, lb_logits, stacked, *, ts):
    b, s, d = x.shape
    n_heads = d // HEAD_DIM
    n_lead = 4
    args = [x] + list(stacked[:n_lead]) + [lb_logits] + list(stacked[n_lead:])
    in_specs = [pl.BlockSpec((None, ts, d), lambda bi, si: (bi, si, 0))]
    in_specs += [_layer_resident(a.shape, layer) for a in stacked[:n_lead]]
    in_specs += [_resident(lb_logits.shape)]
    in_specs += [_layer_resident(a.shape, layer) for a in stacked[n_lead:]]
    return pl.pallas_call(
        functools.partial(_mixer_kernel, layer=layer, n_heads=n_heads),
        out_shape=jax.ShapeDtypeStruct((b, s, d), F32),
        grid=(b, s // ts),
        in_specs=in_specs,
        out_specs=pl.BlockSpec((None, ts, d), lambda bi, si: (bi, si, 0)),
        scratch_shapes=[
            pltpu.VMEM((n_heads, HEAD_DIM, HEAD_DIM), F32),
            pltpu.VMEM((1, d), F32),
            pltpu.VMEM((ts + SUBLANES, d), F32),
            pltpu.VMEM((ts + 4 * SUBLANES, d), F32),
            pltpu.VMEM((ts, d), BF16),
            pltpu.VMEM((ts, d), BF16),
            pltpu.VMEM((ts, d), BF16),
            pltpu.VMEM((ts, d), F32),
            pltpu.VMEM((ts, d), F32),
            pltpu.VMEM((ts, d), F32),
            pltpu.VMEM((ts, d), F32),
            pltpu.VMEM((ts, d), F32),
            pltpu.VMEM((ts, d), F32),
            pltpu.VMEM((ts, d), F32),
        ],
        compiler_params=pltpu.CompilerParams(
            dimension_semantics=("parallel", "arbitrary"), vmem_limit_bytes=VMEM_LIMIT),
        name="mixer",
    )(*args)


def _final_norm_kernel(x_ref, g_ref, o_ref):
    o_ref[...] = _rmsnorm(x_ref[...], g_ref[...])


def _final_norm(x2d, g, *, tm):
    t, d = x2d.shape
    return pl.pallas_call(
        _final_norm_kernel,
        out_shape=jax.ShapeDtypeStruct((t, d), F32),
        grid=(t // tm,),
        in_specs=[pl.BlockSpec((tm, d), lambda i: (i, 0)), _resident((1, d))],
        out_specs=pl.BlockSpec((tm, d), lambda i: (i, 0)),
        compiler_params=pltpu.CompilerParams(dimension_semantics=("parallel",)),
        name="final_norm",
    )(x2d, g)


def _tile(n, want):
    t = min(n, want)
    assert n % t == 0, (n, t)
    return t


def kernel(x, norm_ffn1, ffn1_w_in, ffn1_w_out, norm_mix, w_in_mix, b_in_mix, hgrn_lb_logits, hg_norm,
           lru_conv_w, lru_conv_b, lru_gate_w, lru_gate_b, lru_lambda, cv_dw_w, cv_dw_b, cv_ln_g, cv_ln_b,
           w_branch, w_out_mix, norm_ffn2, ffn2_w_in, ffn2_w_out, norm_final):
    bsz, seq, d = x.shape
    depth = norm_ffn1.shape[0]
    t = bsz * seq
    tm = _tile(t, 512)
    ts = _tile(seq, 256)
    assert ts % HG_CHUNK == 0 and d % HEAD_DIM == 0 and d % MXU_WIDTH == 0
    d_ff = ffn1_w_out.shape[1]
    ff_chunk = 256 if d_ff % 256 == 0 else d_ff

    rows = lambda a: a[:, None, :]
    ffn1 = (rows(norm_ffn1), ffn1_w_in.astype(BF16), ffn1_w_out.astype(BF16))
    ffn2 = (rows(norm_ffn2), ffn2_w_in.astype(BF16), ffn2_w_out.astype(BF16))
    lgw = jnp.concatenate([lru_gate_w[:, 0], lru_gate_w[:, 1]], axis=-1).astype(BF16)
    taps8 = lambda w: jnp.broadcast_to(w[:, :, None, :], w.shape[:2] + (SUBLANES, w.shape[2]))
    w_br = jnp.concatenate([w_branch[:, i] for i in range(w_branch.shape[1])], axis=-1)
    mix = (rows(norm_mix), *_split_cols(w_in_mix.astype(BF16), 2), rows(b_in_mix), rows(hg_norm),
           taps8(lru_conv_w), rows(lru_conv_b), lgw, lru_gate_b, rows(lru_lambda), taps8(cv_dw_w), rows(cv_dw_b),
           rows(cv_ln_g), rows(cv_ln_b),
           *_split_cols(w_br.astype(BF16), 2), *_split_cols(w_out_mix.astype(BF16), 2))

    for l in range(depth):
        x = _ffn(x.reshape(t, d), l, *ffn1, tm=tm, ff_chunk=ff_chunk).reshape(bsz, seq, d)
        x = _mixer(x, l, hgrn_lb_logits, mix, ts=ts)
        x = _ffn(x.reshape(t, d), l, *ffn2, tm=tm, ff_chunk=ff_chunk).reshape(bsz, seq, d)
    return _final_norm(x.reshape(t, d), norm_final.reshape(1, d), tm=tm).reshape(bsz, seq, d)
```

```python
import functools

import jax
import jax.numpy as jnp
from jax import lax
from jax.experimental import pallas as pl
from jax.experimental.pallas import tpu as pltpu

F32 = jnp.float32
BF16 = jnp.bfloat16

SUBLANES = 8
LANES = 128
MXU_WIDTH = 256
HEAD_DIM = 128
HG_CHUNK = 64
LRU_C = 8.0
RMS_EPS = 1e-6
LN_EPS = 1e-5
V7X_VMEM_BYTES = 64 * 1024 * 1024
VMEM_LIMIT = V7X_VMEM_BYTES - 8 * 1024 * 1024


def _dot(a, b):
    return jnp.dot(a, b, preferred_element_type=F32)


def _dot_nt(a, b):
    return lax.dot_general(a, b, (((1,), (1,)), ((), ())), preferred_element_type=F32)


def _dot_tn(a, b):
    return lax.dot_general(a, b, (((0,), (0,)), ((), ())), preferred_element_type=F32)


def _panel(w_refs, c):
    width = w_refs[0].shape[1]
    assert width % MXU_WIDTH == 0
    return w_refs[c // width][:, c % width:c % width + MXU_WIDTH]


def _split_cols(w, n_parts):
    width = w.shape[-1] // n_parts
    assert width * n_parts == w.shape[-1] and width % MXU_WIDTH == 0 and width % 1024 != 0
    return [w[..., i * width:(i + 1) * width] for i in range(n_parts)]


def _sigmoid(x):
    return 1.0 / (1.0 + jnp.exp(-x))


def _silu(x):
    return x * _sigmoid(x)


def _gelu_tanh(x):
    return 0.5 * x * (1.0 + jnp.tanh(0.7978845608028654 * (x + 0.044715 * (x * x * x))))


def _rmsnorm(x, g):
    return x * lax.rsqrt(jnp.mean(x * x, axis=-1, keepdims=True) + RMS_EPS) * g


def _resident(shape):
    nd = len(shape)
    return pl.BlockSpec(shape, lambda *_: (0,) * nd, pipeline_mode=pl.Buffered(1))


def _layer_resident(shape, layer):
    nd = len(shape)
    return pl.BlockSpec((None,) + tuple(shape[1:]), lambda *_: (layer,) + (0,) * (nd - 1),
                        pipeline_mode=pl.Buffered(1))


def _emit_interleaved(vector_tasks, matmul_tasks):
    n, m = len(vector_tasks), len(matmul_tasks)
    i = j = 0
    while i < n or j < m:
        if j >= m or (i < n and i * m <= j * n):
            vector_tasks[i]()
            i += 1
        else:
            matmul_tasks[j]()
            j += 1


def _ffn_kernel(x_ref, g_ref, win_ref, wout_ref, gfin_ref, o_ref, *, d_ff, ff_chunk, final_norm):
    x = x_ref[...]
    xn = _rmsnorm(x, g_ref[...]).astype(BF16)
    acc = jnp.zeros(x.shape, F32)
    for j in range(d_ff // ff_chunk):
        lo = j * ff_chunk
        gate = _dot(xn, win_ref[:, lo:lo + ff_chunk])
        up = _dot(xn, win_ref[:, d_ff + lo:d_ff + lo + ff_chunk])
        h = (_silu(gate) * up).astype(BF16)
        acc = acc + _dot(h, wout_ref[lo:lo + ff_chunk, :])
    out = x + 0.5 * acc
    o_ref[...] = _rmsnorm(out, gfin_ref[...]) if final_norm else out


def _ffn(x2d, layer, g, w_in, w_out, g_final, *, tm, ff_chunk, final_norm):
    t, d = x2d.shape
    d_ff = w_out.shape[1]
    return pl.pallas_call(
        functools.partial(_ffn_kernel, d_ff=d_ff, ff_chunk=ff_chunk, final_norm=final_norm),
        out_shape=jax.ShapeDtypeStruct((t, d), F32),
        grid=(t // tm,),
        in_specs=[
            pl.BlockSpec((tm, d), lambda i: (i, 0)),
            _layer_resident(g.shape, layer),
            _layer_resident(w_in.shape, layer),
            _layer_resident(w_out.shape, layer),
            _resident(g_final.shape),
        ],
        out_specs=pl.BlockSpec((tm, d), lambda i: (i, 0)),
        compiler_params=pltpu.CompilerParams(
            dimension_semantics=("parallel",), vmem_limit_bytes=VMEM_LIMIT),
        name="ffn",
    )(x2d, g, w_in, w_out, g_final)


def _split3(x):
    hi = x.astype(BF16)
    r = x - hi.astype(F32)
    mid = r.astype(BF16)
    lo = (r - mid.astype(F32)).astype(BF16)
    return hi, mid, lo


def _block_mid(b, hs):
    c = b.shape[0]
    blk = 2 * hs

    def rows_every8(off):
        return jnp.concatenate(
            [jnp.broadcast_to(b[r + off:r + off + 1, :], (SUBLANES, b.shape[1]))
             for r in range(0, c, SUBLANES)], axis=0)

    if blk >= SUBLANES:
        return jnp.concatenate(
            [jnp.broadcast_to(b[r + hs - 1:r + hs, :], (blk, b.shape[1])) for r in range(0, c, blk)], axis=0)
    assert blk == 4
    row8 = lax.broadcasted_iota(jnp.int32, b.shape, 0) % SUBLANES
    return jnp.where(row8 < 4, rows_every8(1), rows_every8(5))


def _level_masks(c):
    ri = lax.broadcasted_iota(jnp.int32, (c, c), 0)
    ci = lax.broadcasted_iota(jnp.int32, (c, c), 1)
    masks = []
    hs = c // 2
    while hs >= 1:
        blk = 2 * hs
        masks.append(((ri // blk) == (ci // blk)) & ((ri % blk) >= hs) & ((ci % blk) < hs))
        hs //= 2
    return masks


def _hgrn2_scores(q, k, f, b):
    c = q.shape[0]
    odd = (lax.broadcasted_iota(jnp.int32, q.shape, 0) % 2) == 1
    out = []
    hs = c // 2
    while hs >= 1:
        e = jnp.where(odd, f, 1.0) if hs == 1 else jnp.exp2(-jnp.abs(b - _block_mid(b, hs)))
        out.append(_dot_nt((q * e).astype(BF16), (k * e).astype(BF16)))
        hs //= 2
    return out


def _conv_unit(ext_ref, hist, w_ref, bias_ref, out_ref, *, row_block, lane_block):
    n_tap = w_ref.shape[0]
    base = hist - (n_tap - 1)
    groups = row_block // SUBLANES

    def unit(c0, lanes):
        sub = lax.broadcasted_iota(jnp.int32, (groups, SUBLANES, lane_block), 1)
        acc = jnp.broadcast_to(bias_ref[:, lanes][None], (groups, SUBLANES, lane_block))
        for r in range(SUBLANES):
            taps = [i for i in range(n_tap) if (base + i) % SUBLANES == r]
            if not taps:
                continue
            rows = row_block if r == 0 else row_block + SUBLANES
            p = None
            for i in taps:
                a0 = c0 + base + i - r
                win = ext_ref[a0:a0 + rows, lanes].reshape(rows // SUBLANES, SUBLANES, lane_block)
                term = win * w_ref[i, :, lanes][None]
                p = term if p is None else p + term
            if r:
                pr = pltpu.roll(p, SUBLANES - r, 1)
                p = jnp.where(sub < SUBLANES - r, pr[0:groups], pr[1:groups + 1])
            acc = acc + p
        out_ref[c0:c0 + row_block, lanes] = acc.reshape(row_block, lane_block)

    return unit


def _lru_scan(a_t, u_t, carry):
    ts, n = a_t.shape
    groups = ts // SUBLANES
    a3 = a_t.reshape(groups, SUBLANES, n)
    h3 = u_t.reshape(groups, SUBLANES, n)
    row = lax.broadcasted_iota(jnp.int32, a3.shape, 1)
    sh = 1
    while sh < SUBLANES:
        valid = row >= sh
        h_prev = jnp.where(valid, pltpu.roll(h3, sh, 1), 0.0)
        a_prev = jnp.where(valid, pltpu.roll(a3, sh, 1), 1.0)
        h3 = h3 + a3 * h_prev
        a3 = a3 * a_prev
        sh *= 2
    out = []
    for g in range(groups):
        hg = h3[g] + a3[g] * carry
        out.append(hg)
        carry = hg[SUBLANES - 1:SUBLANES, :]
    return jnp.concatenate(out, axis=0), carry


def _mixer_kernel(x_ref, nrm_ref, win0_ref, win1_ref, bin_ref, lbl_ref, hgn_ref, lcw_ref, lcb_ref, lgw_ref,
                  lgb_ref, lam_ref, cw_ref, cb_ref, lng_ref, lnb_ref, wbr0_ref, wbr1_ref, wout0_ref, wout1_ref,
                  o_ref,
                  st_ref, hl_ref, lxe_ref, cue_ref, xn_ref, y16_ref, a16_ref,
                  q_ref, f_ref, v_ref, b_ref, y_ref, m_ref, cv_ref,
                  *, layer, n_heads):
    ts, d = x_ref.shape
    chunk = HG_CHUNK
    lru_hist, cv_hist = SUBLANES, 4 * SUBLANES
    panels = [slice(c, c + MXU_WIDTH) for c in range(0, d, MXU_WIDTH)]
    win_refs, wbr_refs, wout_refs = (win0_ref, win1_ref), (wbr0_ref, wbr1_ref), (wout0_ref, wout1_ref)

    @pl.when(pl.program_id(1) == 0)
    def _():
        st_ref[...] = jnp.zeros(st_ref.shape, F32)
        hl_ref[...] = jnp.zeros(hl_ref.shape, F32)
        lxe_ref[0:lru_hist, :] = jnp.zeros((lru_hist, d), F32)
        cue_ref[0:cv_hist, :] = jnp.zeros((cv_hist, d), F32)

    xn_ref[...] = _rmsnorm(x_ref[...], nrm_ref[...]).astype(BF16)

    def proj(i, p):
        c = i * d + p.start
        return _dot(xn_ref[...], _panel(win_refs, c)) + bin_ref[:, c:c + MXU_WIDTH]

    def branch_proj(a16_ref_, i, p):
        return _dot(a16_ref_[...], _panel(wbr_refs, i * d + p.start))

    lg = lbl_ref[...]
    ex = jnp.exp(lg - jnp.max(lg, axis=0, keepdims=True))
    sm = ex / jnp.sum(ex, axis=0, keepdims=True)
    lb = jnp.zeros((1, d), F32)
    for i in range(1, layer + 1):
        lb = lb + sm[i:i + 1, :]

    for p in panels:
        lxe_ref[lru_hist:lru_hist + ts, p] = proj(4, p)
    xb_ref = y_ref
    conv_rows = ts // 2
    lru_conv = _conv_unit(lxe_ref, lru_hist, lcw_ref, lcb_ref, xb_ref, row_block=conv_rows, lane_block=LANES)
    cv_conv = _conv_unit(cue_ref, cv_hist, cw_ref, cb_ref, cv_ref, row_block=conv_rows, lane_block=LANES)

    def lru_conv_task(l0):
        lru_conv(0, slice(l0, l0 + LANES))
        lru_conv(conv_rows, slice(l0, l0 + LANES))

    def glu_task(p):
        cue_ref[cv_hist:cv_hist + ts, p] = proj(6, p) * _sigmoid(proj(7, p))

    _emit_interleaved([functools.partial(lru_conv_task, l0) for l0 in range(0, d, LANES)],
                      [functools.partial(glu_task, p) for p in panels])
    lxe_ref[0:lru_hist, :] = lxe_ref[ts:ts + lru_hist, :]

    def q_task(p):
        q_ref[:, p] = _silu(proj(0, p))

    def f_task(p):
        f_ref[:, p] = lb[:, p] + (1.0 - lb[:, p]) * _sigmoid(proj(1, p))

    def v_task(p):
        v_ref[:, p] = proj(2, p)

    def lgate_task(p):
        m_ref[:, p] = _gelu_tanh(proj(5, p))

    lam = -lam_ref[...]
    softplus = jnp.maximum(lam, 0.0) + jnp.log(1.0 + jnp.exp(-jnp.abs(lam)))

    def lru_task(h):
        cols = slice(h * HEAD_DIM, (h + 1) * HEAD_DIM)
        xb = xb_ref[:, cols]
        gates = _dot(xb.astype(BF16), lgw_ref[h])
        gr = gates[:, :HEAD_DIM] + lgb_ref[0:1, cols]
        gi = gates[:, HEAD_DIM:] + lgb_ref[1:2, cols]
        a_t = jnp.exp((-LRU_C) * _sigmoid(gr) * softplus[:, cols])
        u_t = jnp.sqrt(1.0 - a_t * a_t) * (_sigmoid(gi) * xb)
        hh, last = _lru_scan(a_t, u_t, hl_ref[:, cols])
        hl_ref[:, cols] = last
        a16_ref[:, cols] = (hh * m_ref[:, cols]).astype(BF16)

    vector_tasks = [functools.partial(cv_conv, 0, slice(l0, l0 + LANES)) for l0 in range(0, d, LANES)]
    vector_tasks += [functools.partial(lru_task, h) for h in range(d // HEAD_DIM)]
    _emit_interleaved(
        vector_tasks, [functools.partial(t, p) for t in (lgate_task, q_task, f_task, v_task) for p in panels])

    tri = (lax.broadcasted_iota(jnp.int32, (chunk, chunk), 0)
           >= lax.broadcasted_iota(jnp.int32, (chunk, chunk), 1)).astype(BF16)
    for c in range(ts // chunk):
        rows = slice(c * chunk, (c + 1) * chunk)
        g_hi, g_mid, g_lo = _split3(jnp.log2(f_ref[rows, :]))
        b_ref[rows, :] = _dot(tri, g_hi) + _dot(tri, g_mid) + _dot(tri, g_lo)

    hgn = hgn_ref[...]

    def hgrn2_chunk(c):
        rows = slice(c * chunk, (c + 1) * chunk)
        masks = _level_masks(chunk)
        heads = [slice(h * HEAD_DIM, (h + 1) * HEAD_DIM) for h in range(n_heads)]
        scores, carried = [], []
        for h, cols in enumerate(heads):
            q, f, v, b = q_ref[rows, cols], f_ref[rows, cols], v_ref[rows, cols], b_ref[rows, cols]
            k = 1.0 - f
            scores.append(_hgrn2_scores(q, k, f, b))
            st = st_ref[h]
            carried.append(_dot_nt((q * jnp.exp2(b)).astype(BF16), st.astype(BF16)))
            b_last = b[chunk - 1:chunk, :]
            kd = (k * jnp.exp2(b_last - b)).astype(BF16)
            st_ref[h] = st * jnp.exp2(b_last) + _dot_tn(v.astype(BF16), kd)
        for h, cols in enumerate(heads):
            q, f, v = q_ref[rows, cols], f_ref[rows, cols], v_ref[rows, cols]
            a = jnp.zeros((chunk, chunk), F32)
            for m, al in zip(masks, scores[h]):
                a = jnp.where(m, al, a)
            o = _dot(a.astype(BF16), v.astype(BF16)) + carried[h]
            o = o + jnp.sum(q * (1.0 - f), axis=-1, keepdims=True) * v
            y_ref[rows, cols] = _rmsnorm(o, hgn)

    for c in range(ts // chunk):
        hgrn2_chunk(c)

    ga_ref, gb_ref, gc_ref = q_ref, f_ref, v_ref

    def yhg_task(p):
        y16_ref[:, p] = (y_ref[:, p] * _silu(proj(3, p))).astype(BF16)

    def gate_task(ref, i, p):
        ref[:, p] = _sigmoid(proj(i, p))

    matmul_tasks = []
    for p in panels:
        matmul_tasks += [functools.partial(yhg_task, p), functools.partial(gate_task, ga_ref, 8, p),
                         functools.partial(gate_task, gb_ref, 9, p), functools.partial(gate_task, gc_ref, 10, p)]
    _emit_interleaved(
        [functools.partial(cv_conv, conv_rows, slice(l0, l0 + LANES)) for l0 in range(0, d, LANES)], matmul_tasks)
    cue_ref[0:cv_hist, :] = cue_ref[ts:ts + cv_hist, :]

    def ln_task(r0):
        rows = slice(r0, r0 + conv_rows)
        cv = cv_ref[rows, :]
        mu = jnp.mean(cv, axis=-1, keepdims=True)
        xc = cv - mu
        ln = xc * lax.rsqrt(jnp.mean(xc * xc, axis=-1, keepdims=True) + LN_EPS) * lng_ref[...] + lnb_ref[...]
        xn_ref[rows, :] = _silu(ln).astype(BF16)

    def merge_ab_task(p):
        m_ref[:, p] = (ga_ref[:, p] * branch_proj(y16_ref, 0, p)
                       + gb_ref[:, p] * branch_proj(a16_ref, 1, p))

    _emit_interleaved([functools.partial(ln_task, r0) for r0 in range(0, ts, conv_rows)],
                      [functools.partial(merge_ab_task, p) for p in panels])
    for p in panels:
        y16_ref[:, p] = (m_ref[:, p] + gc_ref[:, p] * branch_proj(xn_ref, 2, p)).astype(BF16)
    for p in panels:
        o_ref[:, p] = x_ref[:, p] + _dot(y16_ref[...], _panel(wout_refs, p.start))


def _mixer(x, layer, lb_logits, stacked, *, ts):
    b, s, d = x.shape
    n_heads = d // HEAD_DIM
    n_lead = 4
    args = [x] + list(stacked[:n_lead]) + [lb_logits] + list(stacked[n_lead:])
    in_specs = [pl.BlockSpec((None, ts, d), lambda bi, si: (bi, si, 0))]
    in_specs += [_layer_resident(a.shape, layer) for a in stacked[:n_lead]]
    in_specs += [_resident(lb_logits.shape)]
    in_specs += [_layer_resident(a.shape, layer) for a in stacked[n_lead:]]
    return pl.pallas_call(
        functools.partial(_mixer_kernel, layer=layer, n_heads=n_heads),
        out_shape=jax.ShapeDtypeStruct((b, s, d), F32),
        grid=(b, s // ts),
        in_specs=in_specs,
        out_specs=pl.BlockSpec((None, ts, d), lambda bi, si: (bi, si, 0)),
        scratch_shapes=[
            pltpu.VMEM((n_heads, HEAD_DIM, HEAD_DIM), F32),
            pltpu.VMEM((1, d), F32),
            pltpu.VMEM((ts + SUBLANES, d), F32),
            pltpu.VMEM((ts + 4 * SUBLANES, d), F32),
            pltpu.VMEM((ts, d), BF16),
            pltpu.VMEM((ts, d), BF16),
            pltpu.VMEM((ts, d), BF16),
            pltpu.VMEM((ts, d), F32),
            pltpu.VMEM((ts, d), F32),
            pltpu.VMEM((ts, d), F32),
            pltpu.VMEM((ts, d), F32),
            pltpu.VMEM((ts, d), F32),
            pltpu.VMEM((ts, d), F32),
            pltpu.VMEM((ts, d), F32),
        ],
        compiler_params=pltpu.CompilerParams(
            dimension_semantics=("parallel", "arbitrary"), vmem_limit_bytes=VMEM_LIMIT),
        name="mixer",
    )(*args)


def _tile(n, want):
    t = min(n, want)
    assert n % t == 0, (n, t)
    return t


def kernel(x, norm_ffn1, ffn1_w_in, ffn1_w_out, norm_mix, w_in_mix, b_in_mix, hgrn_lb_logits, hg_norm,
           lru_conv_w, lru_conv_b, lru_gate_w, lru_gate_b, lru_lambda, cv_dw_w, cv_dw_b, cv_ln_g, cv_ln_b,
           w_branch, w_out_mix, norm_ffn2, ffn2_w_in, ffn2_w_out, norm_final):
    bsz, seq, d = x.shape
    depth = norm_ffn1.shape[0]
    t = bsz * seq
    tm = _tile(t, 512)
    ts = _tile(seq, 256)
    assert ts % HG_CHUNK == 0 and d % HEAD_DIM == 0 and d % MXU_WIDTH == 0
    d_ff = ffn1_w_out.shape[1]
    ff_chunk = 256 if d_ff % 256 == 0 else d_ff

    rows = lambda a: a[:, None, :]
    ffn1 = (rows(norm_ffn1), ffn1_w_in.astype(BF16), ffn1_w_out.astype(BF16))
    ffn2 = (rows(norm_ffn2), ffn2_w_in.astype(BF16), ffn2_w_out.astype(BF16))
    lgw = jnp.concatenate([lru_gate_w[:, 0], lru_gate_w[:, 1]], axis=-1).astype(BF16)
    taps8 = lambda w: jnp.broadcast_to(w[:, :, None, :], w.shape[:2] + (SUBLANES, w.shape[2]))
    w_br = jnp.concatenate([w_branch[:, i] for i in range(w_branch.shape[1])], axis=-1)
    mix = (rows(norm_mix), *_split_cols(w_in_mix.astype(BF16), 2), rows(b_in_mix), rows(hg_norm),
           taps8(lru_conv_w), rows(lru_conv_b), lgw, lru_gate_b, rows(lru_lambda), taps8(cv_dw_w), rows(cv_dw_b),
           rows(cv_ln_g), rows(cv_ln_b),
           *_split_cols(w_br.astype(BF16), 2), *_split_cols(w_out_mix.astype(BF16), 2))

    g_final = norm_final.reshape(1, d)
    for l in range(depth):
        x = _ffn(x.reshape(t, d), l, *ffn1, g_final, tm=tm, ff_chunk=ff_chunk, final_norm=False)
        x = _mixer(x.reshape(bsz, seq, d), l, hgrn_lb_logits, mix, ts=ts)
        x = _ffn(x.reshape(t, d), l, *ffn2, g_final, tm=tm, ff_chunk=ff_chunk, final_norm=(l == depth - 1))
    return x.reshape(bsz, seq, d)
```

```python
import functools

import jax
import jax.numpy as jnp
from jax import lax
from jax.experimental import pallas as pl
from jax.experimental.pallas import tpu as pltpu

F32 = jnp.float32
BF16 = jnp.bfloat16

SUBLANES = 8
LANES = 128
MXU_WIDTH = 256
LOG2E = 1.4426950408889634
HEAD_DIM = 128
HG_CHUNK = 64
LRU_C = 8.0
RMS_EPS = 1e-6
LN_EPS = 1e-5
V7X_VMEM_BYTES = 64 * 1024 * 1024
VMEM_LIMIT = V7X_VMEM_BYTES - 8 * 1024 * 1024


def _dot(a, b):
    return jnp.dot(a, b, preferred_element_type=F32)


def _dot_nt(a, b):
    return lax.dot_general(a, b, (((1,), (1,)), ((), ())), preferred_element_type=F32)


def _dot_tn(a, b):
    return lax.dot_general(a, b, (((0,), (0,)), ((), ())), preferred_element_type=F32)


def _panel(w_refs, c):
    width = w_refs[0].shape[1]
    assert width % MXU_WIDTH == 0
    return w_refs[c // width][:, c % width:c % width + MXU_WIDTH]


def _sigmoid(x):
    return 1.0 / (1.0 + jnp.exp2(x * (-LOG2E)))


def _silu(x):
    return x * _sigmoid(x)


def _gelu_tanh(x):
    return 0.5 * x * (1.0 + jnp.tanh(0.7978845608028654 * (x + 0.044715 * (x * x * x))))


def _rmsnorm(x, g):
    return x * lax.rsqrt(jnp.mean(x * x, axis=-1, keepdims=True) + RMS_EPS) * g


def _resident(shape):
    nd = len(shape)
    return pl.BlockSpec(shape, lambda *_: (0,) * nd, pipeline_mode=pl.Buffered(1))


def _layer_resident(shape, layer):
    nd = len(shape)
    return pl.BlockSpec((None,) + tuple(shape[1:]), lambda *_: (layer,) + (0,) * (nd - 1),
                        pipeline_mode=pl.Buffered(1))


def _weight_halves(w, lead):
    k, n = w.shape[-2:]
    half = n // 2
    assert 2 * half == n and half % MXU_WIDTH == 0 and half % 1024 != 0
    block = (None,) * len(lead) + (k, half)
    return [(w, pl.BlockSpec(block, functools.partial(lambda part, *_: tuple(lead) + (0, part), part),
                             pipeline_mode=pl.Buffered(1)))
            for part in range(2)]


def _emit_interleaved(vector_tasks, matmul_tasks):
    n, m = len(vector_tasks), len(matmul_tasks)
    i = j = 0
    while i < n or j < m:
        if j >= m or (i < n and i * m <= j * n):
            vector_tasks[i]()
            i += 1
        else:
            matmul_tasks[j]()
            j += 1


def _ffn_kernel(x_ref, g_ref, win_ref, wout_ref, gfin_ref, o_ref, *, d_ff, ff_chunk, final_norm):
    x = x_ref[...]
    xn = _rmsnorm(x, g_ref[...]).astype(BF16)
    acc = jnp.zeros(x.shape, F32)
    for j in range(d_ff // ff_chunk):
        lo = j * ff_chunk
        gate = _dot(xn, win_ref[:, lo:lo + ff_chunk].astype(BF16))
        up = _dot(xn, win_ref[:, d_ff + lo:d_ff + lo + ff_chunk].astype(BF16))
        h = (_silu(gate) * up).astype(BF16)
        acc = acc + _dot(h, wout_ref[lo:lo + ff_chunk, :].astype(BF16))
    out = x + 0.5 * acc
    o_ref[...] = _rmsnorm(out, gfin_ref[...]) if final_norm else out


def _ffn(x2d, layer, g, w_in, w_out, g_final, *, tm, ff_chunk, final_norm):
    t, d = x2d.shape
    d_ff = w_out.shape[1]
    return pl.pallas_call(
        functools.partial(_ffn_kernel, d_ff=d_ff, ff_chunk=ff_chunk, final_norm=final_norm),
        out_shape=jax.ShapeDtypeStruct((t, d), F32),
        grid=(t // tm,),
        in_specs=[
            pl.BlockSpec((tm, d), lambda i: (i, 0)),
            _layer_resident(g.shape, layer),
            _layer_resident(w_in.shape, layer),
            _layer_resident(w_out.shape, layer),
            _resident(g_final.shape),
        ],
        out_specs=pl.BlockSpec((tm, d), lambda i: (i, 0)),
        compiler_params=pltpu.CompilerParams(
            dimension_semantics=("parallel",), vmem_limit_bytes=VMEM_LIMIT),
        name="ffn",
    )(x2d, g, w_in, w_out, g_final)


def _split3(x):
    hi = x.astype(BF16)
    r = x - hi.astype(F32)
    mid = r.astype(BF16)
    lo = (r - mid.astype(F32)).astype(BF16)
    return hi, mid, lo


def _block_mid(b, hs):
    c = b.shape[0]
    blk = 2 * hs

    def rows_every8(off):
        return jnp.concatenate(
            [jnp.broadcast_to(b[r + off:r + off + 1, :], (SUBLANES, b.shape[1]))
             for r in range(0, c, SUBLANES)], axis=0)

    if blk >= SUBLANES:
        return jnp.concatenate(
            [jnp.broadcast_to(b[r + hs - 1:r + hs, :], (blk, b.shape[1])) for r in range(0, c, blk)], axis=0)
    assert blk == 4
    row8 = lax.broadcasted_iota(jnp.int32, b.shape, 0) % SUBLANES
    return jnp.where(row8 < 4, rows_every8(1), rows_every8(5))


def _level_masks(c):
    ri = lax.broadcasted_iota(jnp.int32, (c, c), 0)
    ci = lax.broadcasted_iota(jnp.int32, (c, c), 1)
    masks = []
    hs = c // 2
    while hs >= 1:
        blk = 2 * hs
        masks.append(((ri // blk) == (ci // blk)) & ((ri % blk) >= hs) & ((ci % blk) < hs))
        hs //= 2
    return masks


def _hgrn2_scores(q, k, f, b):
    c = q.shape[0]
    odd = (lax.broadcasted_iota(jnp.int32, q.shape, 0) % 2) == 1
    out = []
    hs = c // 2
    while hs >= 1:
        e = jnp.where(odd, f, 1.0) if hs == 1 else jnp.exp2(-jnp.abs(b - _block_mid(b, hs)))
        out.append(_dot_nt((q * e).astype(BF16), (k * e).astype(BF16)))
        hs //= 2
    return out


def _conv_unit(ext_ref, hist, w_ref, bias_ref, out_ref, *, row_block, lane_block):
    n_tap = w_ref.shape[0]
    base = hist - (n_tap - 1)
    groups = row_block // SUBLANES

    def unit(c0, lanes):
        sub = lax.broadcasted_iota(jnp.int32, (groups, SUBLANES, lane_block), 1)
        acc = jnp.broadcast_to(bias_ref[:, lanes][None], (groups, SUBLANES, lane_block))
        for r in range(SUBLANES):
            taps = [i for i in range(n_tap) if (base + i) % SUBLANES == r]
            if not taps:
                continue
            rows = row_block if r == 0 else row_block + SUBLANES
            p = None
            for i in taps:
                a0 = c0 + base + i - r
                win = ext_ref[a0:a0 + rows, lanes].reshape(rows // SUBLANES, SUBLANES, lane_block)
                term = win * w_ref[i, :, lanes][None]
                p = term if p is None else p + term
            if r:
                pr = pltpu.roll(p, SUBLANES - r, 1)
                p = jnp.where(sub < SUBLANES - r, pr[0:groups], pr[1:groups + 1])
            acc = acc + p
        out_ref[c0:c0 + row_block, lanes] = acc.reshape(row_block, lane_block)

    return unit


def _lru_scan(a_t, u_t, carry):
    ts, n = a_t.shape
    groups = ts // SUBLANES
    a3 = a_t.reshape(groups, SUBLANES, n)
    h3 = u_t.reshape(groups, SUBLANES, n)
    row = lax.broadcasted_iota(jnp.int32, a3.shape, 1)
    sh = 1
    while sh < SUBLANES:
        valid = row >= sh
        h_prev = jnp.where(valid, pltpu.roll(h3, sh, 1), 0.0)
        a_prev = jnp.where(valid, pltpu.roll(a3, sh, 1), 1.0)
        h3 = h3 + a3 * h_prev
        a3 = a3 * a_prev
        sh *= 2
    out = []
    for g in range(groups):
        hg = h3[g] + a3[g] * carry
        out.append(hg)
        carry = hg[SUBLANES - 1:SUBLANES, :]
    return jnp.concatenate(out, axis=0), carry


def _mixer_kernel(x_ref, nrm_ref, win0_ref, win1_ref, bin_ref, lbl_ref, hgn_ref, lcw_ref, lcb_ref, lgw_ref,
                  lgb_ref, lam_ref, cw_ref, cb_ref, lng_ref, lnb_ref,
                  wbr0_ref, wbr1_ref, wbr2_ref, wbr3_ref, wbr4_ref, wbr5_ref, wout0_ref, wout1_ref,
                  o_ref,
                  st_ref, hl_ref, lxe_ref, cue_ref, xn_ref, y16_ref, a16_ref,
                  q_ref, f_ref, v_ref, b_ref, y_ref, m_ref, cv_ref,
                  *, layer, n_heads):
    ts, d = x_ref.shape
    chunk = HG_CHUNK
    lru_hist, cv_hist = SUBLANES, 4 * SUBLANES
    panels = [slice(c, c + MXU_WIDTH) for c in range(0, d, MXU_WIDTH)]
    win_refs, wout_refs = (win0_ref, win1_ref), (wout0_ref, wout1_ref)
    wbr_refs = (wbr0_ref, wbr1_ref, wbr2_ref, wbr3_ref, wbr4_ref, wbr5_ref)

    @pl.when(pl.program_id(1) == 0)
    def _():
        st_ref[...] = jnp.zeros(st_ref.shape, F32)
        hl_ref[...] = jnp.zeros(hl_ref.shape, F32)
        lxe_ref[0:lru_hist, :] = jnp.zeros((lru_hist, d), F32)
        cue_ref[0:cv_hist, :] = jnp.zeros((cv_hist, d), F32)

    def stage(body):
        body()

    xn_ref[...] = _rmsnorm(x_ref[...], nrm_ref[...]).astype(BF16)

    def proj(i, p):
        c = i * d + p.start
        return _dot(xn_ref[...], _panel(win_refs, c)) + bin_ref[:, c:c + MXU_WIDTH]

    def branch_proj(a16_ref_, i, p):
        return _dot(a16_ref_[...], _panel(wbr_refs, i * d + p.start))

    lg = lbl_ref[...]
    ex = jnp.exp(lg - jnp.max(lg, axis=0, keepdims=True))
    sm = ex / jnp.sum(ex, axis=0, keepdims=True)
    lb = jnp.zeros((1, d), F32)
    for i in range(1, layer + 1):
        lb = lb + sm[i:i + 1, :]

    for p in panels:
        lxe_ref[lru_hist:lru_hist + ts, p] = proj(4, p)
    xb_ref = y_ref
    conv_rows = ts // 2
    lru_conv = _conv_unit(lxe_ref, lru_hist, lcw_ref, lcb_ref, xb_ref, row_block=conv_rows, lane_block=LANES)
    cv_conv = _conv_unit(cue_ref, cv_hist, cw_ref, cb_ref, cv_ref, row_block=conv_rows, lane_block=LANES)

    def lru_conv_task(l0):
        lru_conv(0, slice(l0, l0 + LANES))
        lru_conv(conv_rows, slice(l0, l0 + LANES))

    def glu_task(p):
        cue_ref[cv_hist:cv_hist + ts, p] = proj(6, p) * _sigmoid(proj(7, p))

    @stage
    def _():
        _emit_interleaved([functools.partial(lru_conv_task, l0) for l0 in range(0, d, LANES)],
                          [functools.partial(glu_task, p) for p in panels])
        lxe_ref[0:lru_hist, :] = lxe_ref[ts:ts + lru_hist, :]

    def q_task(p):
        q_ref[:, p] = _silu(proj(0, p))

    def f_task(p):
        f_ref[:, p] = lb[:, p] + (1.0 - lb[:, p]) * _sigmoid(proj(1, p))

    def v_task(p):
        v_ref[:, p] = proj(2, p)

    def lgate_task(p):
        m_ref[:, p] = _gelu_tanh(proj(5, p))

    lam = -lam_ref[...]
    softplus = jnp.maximum(lam, 0.0) + jnp.log(1.0 + jnp.exp(-jnp.abs(lam)))

    def lru_task(h):
        cols = slice(h * HEAD_DIM, (h + 1) * HEAD_DIM)
        xb = xb_ref[:, cols]
        gates = _dot(xb.astype(BF16), lgw_ref[h])
        gr = gates[:, :HEAD_DIM] + lgb_ref[0:1, cols]
        gi = gates[:, HEAD_DIM:] + lgb_ref[1:2, cols]
        a_t = jnp.exp((-LRU_C) * _sigmoid(gr) * softplus[:, cols])
        u_t = jnp.sqrt(1.0 - a_t * a_t) * (_sigmoid(gi) * xb)
        hh, last = _lru_scan(a_t, u_t, hl_ref[:, cols])
        hl_ref[:, cols] = last
        a16_ref[:, cols] = (hh * m_ref[:, cols]).astype(BF16)

    vector_tasks = [functools.partial(cv_conv, 0, slice(l0, l0 + LANES)) for l0 in range(0, d, LANES)]
    vector_tasks += [functools.partial(lru_task, h) for h in range(d // HEAD_DIM)]
    @stage
    def _():
        _emit_interleaved(
            vector_tasks, [functools.partial(t, p) for t in (lgate_task, q_task, f_task, v_task) for p in panels])

    def cumsum_chunk(c):
        tri = (lax.broadcasted_iota(jnp.int32, (chunk, chunk), 0)
               >= lax.broadcasted_iota(jnp.int32, (chunk, chunk), 1)).astype(BF16)
        rows = slice(c * chunk, (c + 1) * chunk)
        g_hi, g_mid, g_lo = _split3(jnp.log2(f_ref[rows, :]))
        b_ref[rows, :] = _dot(tri, g_hi) + _dot(tri, g_mid) + _dot(tri, g_lo)


    def hgrn2_chunk(c):
        rows = slice(c * chunk, (c + 1) * chunk)
        masks = _level_masks(chunk)
        heads = [slice(h * HEAD_DIM, (h + 1) * HEAD_DIM) for h in range(n_heads)]
        scores, carried = [], []
        for h, cols in enumerate(heads):
            q, f, v, b = q_ref[rows, cols], f_ref[rows, cols], v_ref[rows, cols], b_ref[rows, cols]
            k = 1.0 - f
            scores.append(_hgrn2_scores(q, k, f, b))
            st = st_ref[h]
            carried.append(_dot_nt((q * jnp.exp2(b)).astype(BF16), st.astype(BF16)))
            b_last = b[chunk - 1:chunk, :]
            kd = (k * jnp.exp2(b_last - b)).astype(BF16)
            st_ref[h] = st * jnp.exp2(b_last) + _dot_tn(v.astype(BF16), kd)
        for h, cols in enumerate(heads):
            q, f, v = q_ref[rows, cols], f_ref[rows, cols], v_ref[rows, cols]
            a = jnp.zeros((chunk, chunk), F32)
            for m, al in zip(masks, scores[h]):
                a = jnp.where(m, al, a)
            o = _dot(a.astype(BF16), v.astype(BF16)) + carried[h]
            o = o + jnp.sum(q * (1.0 - f), axis=-1, keepdims=True) * v
            y_ref[rows, cols] = _rmsnorm(o, hgn_ref[...])

    @stage
    def _():
        for c in range(ts // chunk):
            cumsum_chunk(c)
        for c in range(ts // chunk):
            hgrn2_chunk(c)

    ga_ref, gb_ref, gc_ref = q_ref, f_ref, v_ref

    def yhg_task(p):
        y16_ref[:, p] = (y_ref[:, p] * _silu(proj(3, p))).astype(BF16)

    def gate_task(ref, i, p):
        ref[:, p] = _sigmoid(proj(i, p))

    def merge_ab_task(p):
        m_ref[:, p] = (ga_ref[:, p] * branch_proj(y16_ref, 0, p)
                       + gb_ref[:, p] * branch_proj(a16_ref, 1, p))

    matmul_tasks = [functools.partial(yhg_task, p) for p in panels]
    matmul_tasks += [functools.partial(gate_task, ga_ref, 8, p) for p in panels]
    matmul_tasks += [functools.partial(gate_task, gb_ref, 9, p) for p in panels]
    matmul_tasks += [functools.partial(merge_ab_task, p) for p in panels]
    matmul_tasks += [functools.partial(gate_task, gc_ref, 10, p) for p in panels]

    @stage
    def _():
        _emit_interleaved(
            [functools.partial(cv_conv, conv_rows, slice(l0, l0 + LANES)) for l0 in range(0, d, LANES)],
            matmul_tasks)
        cue_ref[0:cv_hist, :] = cue_ref[ts:ts + cv_hist, :]

    def ln_task(r0):
        rows = slice(r0, r0 + conv_rows)
        cv = cv_ref[rows, :]
        mu = jnp.mean(cv, axis=-1, keepdims=True)
        xc = cv - mu
        ln = xc * lax.rsqrt(jnp.mean(xc * xc, axis=-1, keepdims=True) + LN_EPS) * lng_ref[...] + lnb_ref[...]
        xn_ref[rows, :] = _silu(ln).astype(BF16)

    @stage
    def _():
        for r0 in range(0, ts, conv_rows):
            ln_task(r0)
        for p in panels:
            y16_ref[:, p] = (m_ref[:, p] + gc_ref[:, p] * branch_proj(xn_ref, 2, p)).astype(BF16)
        for p in panels:
            o_ref[:, p] = x_ref[:, p] + _dot(y16_ref[...], _panel(wout_refs, p.start))


def _mixer(x, layer, params, *, ts):
    b, s, d = x.shape
    n_heads = d // HEAD_DIM
    pairs = [p if isinstance(p, tuple) else (p, _layer_resident(p.shape, layer)) for p in params]
    args = [x] + [a for a, _ in pairs]
    in_specs = [pl.BlockSpec((None, ts, d), lambda bi, si: (bi, si, 0))] + [spec for _, spec in pairs]
    return pl.pallas_call(
        functools.partial(_mixer_kernel, layer=layer, n_heads=n_heads),
        out_shape=jax.ShapeDtypeStruct((b, s, d), F32),
        grid=(b, s // ts),
        in_specs=in_specs,
        out_specs=pl.BlockSpec((None, ts, d), lambda bi, si: (bi, si, 0)),
        scratch_shapes=[
            pltpu.VMEM((n_heads, HEAD_DIM, HEAD_DIM), F32),
            pltpu.VMEM((1, d), F32),
            pltpu.VMEM((ts + SUBLANES, d), F32),
            pltpu.VMEM((ts + 4 * SUBLANES, d), F32),
            pltpu.VMEM((ts, d), BF16),
            pltpu.VMEM((ts, d), BF16),
            pltpu.VMEM((ts, d), BF16),
            pltpu.VMEM((ts, d), F32),
            pltpu.VMEM((ts, d), F32),
            pltpu.VMEM((ts, d), F32),
            pltpu.VMEM((ts, d), F32),
            pltpu.VMEM((ts, d), F32),
            pltpu.VMEM((ts, d), F32),
            pltpu.VMEM((ts, d), F32),
        ],
        compiler_params=pltpu.CompilerParams(
            dimension_semantics=("parallel", "arbitrary"), vmem_limit_bytes=VMEM_LIMIT),
        name="mixer",
    )(*args)


def _tile(n, want):
    t = min(n, want)
    assert n % t == 0, (n, t)
    return t


def kernel(x, norm_ffn1, ffn1_w_in, ffn1_w_out, norm_mix, w_in_mix, b_in_mix, hgrn_lb_logits, hg_norm,
           lru_conv_w, lru_conv_b, lru_gate_w, lru_gate_b, lru_lambda, cv_dw_w, cv_dw_b, cv_ln_g, cv_ln_b,
           w_branch, w_out_mix, norm_ffn2, ffn2_w_in, ffn2_w_out, norm_final):
    bsz, seq, d = x.shape
    depth = norm_ffn1.shape[0]
    t = bsz * seq
    tm = _tile(t, 512)
    ts = _tile(seq, 256)
    assert ts % HG_CHUNK == 0 and d % HEAD_DIM == 0 and d % MXU_WIDTH == 0
    d_ff = ffn1_w_out.shape[1]
    ff_chunk = 256 if d_ff % 256 == 0 else d_ff

    rows = lambda a: a[:, None, :]
    ffn1 = (rows(norm_ffn1), ffn1_w_in, ffn1_w_out)
    ffn2 = (rows(norm_ffn2), ffn2_w_in, ffn2_w_out)
    lgw = jnp.concatenate([lru_gate_w[:, 0], lru_gate_w[:, 1]], axis=-1).astype(BF16)
    taps8 = lambda w: jnp.broadcast_to(w[:, :, None, :], w.shape[:2] + (SUBLANES, w.shape[2]))
    w_in16, w_br16, w_out16 = w_in_mix.astype(BF16), w_branch.astype(BF16), w_out_mix.astype(BF16)

    def mixer_params(l):
        branch_halves = [h for i in range(w_br16.shape[1]) for h in _weight_halves(w_br16, (l, i))]
        return [rows(norm_mix), *_weight_halves(w_in16, (l,)), rows(b_in_mix),
                (hgrn_lb_logits, _resident(hgrn_lb_logits.shape)), rows(hg_norm),
                taps8(lru_conv_w), rows(lru_conv_b), lgw, lru_gate_b, rows(lru_lambda), taps8(cv_dw_w),
                rows(cv_dw_b), rows(cv_ln_g), rows(cv_ln_b), *branch_halves, *_weight_halves(w_out16, (l,))]

    g_final = norm_final.reshape(1, d)
    for l in range(depth):
        x = _ffn(x.reshape(t, d), l, *ffn1, g_final, tm=tm, ff_chunk=ff_chunk, final_norm=False)
        x = _mixer(x.reshape(bsz, seq, d), l, mixer_params(l), ts=ts)
        x = _ffn(x.reshape(t, d), l, *ffn2, g_final, tm=tm, ff_chunk=ff_chunk, final_norm=(l == depth - 1))
    return x.reshape(bsz, seq, d)
```

```python
import functools

import jax
import jax.numpy as jnp
from jax import lax
from jax.experimental import pallas as pl
from jax.experimental.pallas import tpu as pltpu

F32 = jnp.float32
BF16 = jnp.bfloat16

SUBLANES = 8
LANES = 128
MXU_WIDTH = 256
LOG2E = 1.4426950408889634
HEAD_DIM = 128
HG_CHUNK = 32
LRU_C = 8.0
RMS_EPS = 1e-6
LN_EPS = 1e-5
V7X_VMEM_BYTES = 64 * 1024 * 1024
VMEM_LIMIT = V7X_VMEM_BYTES - 8 * 1024 * 1024


def _dot(a, b):
    return jnp.dot(a, b, preferred_element_type=F32)


def _dot_nt(a, b):
    return lax.dot_general(a, b, (((1,), (1,)), ((), ())), preferred_element_type=F32)


def _dot_tn(a, b):
    return lax.dot_general(a, b, (((0,), (0,)), ((), ())), preferred_element_type=F32)


def _panel(w_refs, c):
    width = w_refs[0].shape[1]
    assert width % MXU_WIDTH == 0
    return w_refs[c // width][:, c % width:c % width + MXU_WIDTH]


def _sigmoid(x):
    return 1.0 / (1.0 + jnp.exp2(x * (-LOG2E)))


def _silu(x):
    return x * _sigmoid(x)


def _gelu_tanh(x):
    return 0.5 * x * (1.0 + jnp.tanh(0.7978845608028654 * (x + 0.044715 * (x * x * x))))


def _rmsnorm(x, g):
    return x * lax.rsqrt(jnp.mean(x * x, axis=-1, keepdims=True) + RMS_EPS) * g


def _resident(shape):
    nd = len(shape)
    return pl.BlockSpec(shape, lambda *_: (0,) * nd, pipeline_mode=pl.Buffered(1))


def _layer_resident(shape, layer):
    nd = len(shape)
    return pl.BlockSpec((None,) + tuple(shape[1:]), lambda *_: (layer,) + (0,) * (nd - 1),
                        pipeline_mode=pl.Buffered(1))


def _weight_halves(w, lead):
    k, n = w.shape[-2:]
    half = n // 2
    assert 2 * half == n and half % MXU_WIDTH == 0 and half % 1024 != 0
    block = (None,) * len(lead) + (k, half)
    return [(w, pl.BlockSpec(block, functools.partial(lambda part, *_: tuple(lead) + (0, part), part),
                             pipeline_mode=pl.Buffered(1)))
            for part in range(2)]


def _emit_interleaved(vector_tasks, matmul_tasks):
    n, m = len(vector_tasks), len(matmul_tasks)
    i = j = 0
    while i < n or j < m:
        if j >= m or (i < n and i * m <= j * n):
            vector_tasks[i]()
            i += 1
        else:
            matmul_tasks[j]()
            j += 1


def _ffn_kernel(x_ref, g_ref, win_ref, wout_ref, gfin_ref, o_ref, *, d_ff, ff_chunk, final_norm):
    x = x_ref[...]
    xn = _rmsnorm(x, g_ref[...]).astype(BF16)
    acc = jnp.zeros(x.shape, F32)
    for j in range(d_ff // ff_chunk):
        lo = j * ff_chunk
        gate = _dot(xn, win_ref[:, lo:lo + ff_chunk].astype(BF16))
        up = _dot(xn, win_ref[:, d_ff + lo:d_ff + lo + ff_chunk].astype(BF16))
        h = (_silu(gate) * up).astype(BF16)
        acc = acc + _dot(h, wout_ref[lo:lo + ff_chunk, :].astype(BF16))
    out = x + 0.5 * acc
    o_ref[...] = _rmsnorm(out, gfin_ref[...]) if final_norm else out


def _ffn(x2d, layer, g, w_in, w_out, g_final, *, tm, ff_chunk, final_norm):
    t, d = x2d.shape
    d_ff = w_out.shape[1]
    return pl.pallas_call(
        functools.partial(_ffn_kernel, d_ff=d_ff, ff_chunk=ff_chunk, final_norm=final_norm),
        out_shape=jax.ShapeDtypeStruct((t, d), F32),
        grid=(t // tm,),
        in_specs=[
            pl.BlockSpec((tm, d), lambda i: (i, 0)),
            _layer_resident(g.shape, layer),
            _layer_resident(w_in.shape, layer),
            _layer_resident(w_out.shape, layer),
            _resident(g_final.shape),
        ],
        out_specs=pl.BlockSpec((tm, d), lambda i: (i, 0)),
        compiler_params=pltpu.CompilerParams(
            dimension_semantics=("parallel",), vmem_limit_bytes=VMEM_LIMIT),
        name="ffn",
    )(x2d, g, w_in, w_out, g_final)


def _split3(x):
    hi = x.astype(BF16)
    r = x - hi.astype(F32)
    mid = r.astype(BF16)
    lo = (r - mid.astype(F32)).astype(BF16)
    return hi, mid, lo


def _block_mid(b, hs):
    c = b.shape[0]
    blk = 2 * hs

    def rows_every8(off):
        return jnp.concatenate(
            [jnp.broadcast_to(b[r + off:r + off + 1, :], (SUBLANES, b.shape[1]))
             for r in range(0, c, SUBLANES)], axis=0)

    if blk >= SUBLANES:
        return jnp.concatenate(
            [jnp.broadcast_to(b[r + hs - 1:r + hs, :], (blk, b.shape[1])) for r in range(0, c, blk)], axis=0)
    assert blk == 4
    row8 = lax.broadcasted_iota(jnp.int32, b.shape, 0) % SUBLANES
    return jnp.where(row8 < 4, rows_every8(1), rows_every8(5))


def _level_masks(c):
    ri = lax.broadcasted_iota(jnp.int32, (c, c), 0)
    ci = lax.broadcasted_iota(jnp.int32, (c, c), 1)
    masks = []
    hs = c // 2
    while hs >= 1:
        blk = 2 * hs
        masks.append(((ri // blk) == (ci // blk)) & ((ri % blk) >= hs) & ((ci % blk) < hs))
        hs //= 2
    return masks


def _hgrn2_scores(q, k, f, b):
    c = q.shape[0]
    odd = (lax.broadcasted_iota(jnp.int32, q.shape, 0) % 2) == 1
    out = []
    hs = c // 2
    while hs >= 1:
        e = jnp.where(odd, f, 1.0) if hs == 1 else jnp.exp2(-jnp.abs(b - _block_mid(b, hs)))
        out.append(_dot_nt((q * e).astype(BF16), (k * e).astype(BF16)))
        hs //= 2
    return out


def _conv_unit(ext_ref, hist, w_ref, bias_ref, out_ref, *, row_block, lane_block):
    n_tap = w_ref.shape[0]
    base = hist - (n_tap - 1)
    groups = row_block // SUBLANES

    def unit(c0, lanes):
        sub = lax.broadcasted_iota(jnp.int32, (groups, SUBLANES, lane_block), 1)
        acc = jnp.broadcast_to(bias_ref[:, lanes][None], (groups, SUBLANES, lane_block))
        for r in range(SUBLANES):
            taps = [i for i in range(n_tap) if (base + i) % SUBLANES == r]
            if not taps:
                continue
            rows = row_block if r == 0 else row_block + SUBLANES
            p = None
            for i in taps:
                a0 = c0 + base + i - r
                win = ext_ref[a0:a0 + rows, lanes].reshape(rows // SUBLANES, SUBLANES, lane_block)
                term = win * w_ref[i, :, lanes][None]
                p = term if p is None else p + term
            if r:
                pr = pltpu.roll(p, SUBLANES - r, 1)
                p = jnp.where(sub < SUBLANES - r, pr[0:groups], pr[1:groups + 1])
            acc = acc + p
        out_ref[c0:c0 + row_block, lanes] = acc.reshape(row_block, lane_block)

    return unit


def _lru_scan(a_t, u_t, carry):
    ts, n = a_t.shape
    groups = ts // SUBLANES
    a3 = a_t.reshape(groups, SUBLANES, n)
    h3 = u_t.reshape(groups, SUBLANES, n)
    row = lax.broadcasted_iota(jnp.int32, a3.shape, 1)
    sh = 1
    while sh < SUBLANES:
        valid = row >= sh
        h_prev = jnp.where(valid, pltpu.roll(h3, sh, 1), 0.0)
        a_prev = jnp.where(valid, pltpu.roll(a3, sh, 1), 1.0)
        h3 = h3 + a3 * h_prev
        a3 = a3 * a_prev
        sh *= 2
    out = []
    for g in range(groups):
        hg = h3[g] + a3[g] * carry
        out.append(hg)
        carry = hg[SUBLANES - 1:SUBLANES, :]
    return jnp.concatenate(out, axis=0), carry


def _mixer_kernel(x_ref, nrm_ref, win0_ref, win1_ref, bin_ref, lbl_ref, hgn_ref, lcw_ref, lcb_ref, lgw_ref,
                  lgb_ref, lam_ref, cw_ref, cb_ref, lng_ref, lnb_ref,
                  wbr0_ref, wbr1_ref, wbr2_ref, wbr3_ref, wbr4_ref, wbr5_ref, wout0_ref, wout1_ref,
                  o_ref,
                  st_ref, hl_ref, lxe_ref, cue_ref, xn_ref, y16_ref, a16_ref,
                  q_ref, f_ref, v_ref, b_ref, y_ref, m_ref, cv_ref,
                  *, layer, n_heads):
    ts, d = x_ref.shape
    chunk = HG_CHUNK
    lru_hist, cv_hist = SUBLANES, 4 * SUBLANES
    panels = [slice(c, c + MXU_WIDTH) for c in range(0, d, MXU_WIDTH)]
    win_refs, wout_refs = (win0_ref, win1_ref), (wout0_ref, wout1_ref)
    wbr_refs = (wbr0_ref, wbr1_ref, wbr2_ref, wbr3_ref, wbr4_ref, wbr5_ref)

    @pl.when(pl.program_id(1) == 0)
    def _():
        st_ref[...] = jnp.zeros(st_ref.shape, F32)
        hl_ref[...] = jnp.zeros(hl_ref.shape, F32)
        lxe_ref[0:lru_hist, :] = jnp.zeros((lru_hist, d), F32)
        cue_ref[0:cv_hist, :] = jnp.zeros((cv_hist, d), F32)

    def stage(body):
        body()

    xn_ref[...] = _rmsnorm(x_ref[...], nrm_ref[...]).astype(BF16)

    def proj(i, p):
        c = i * d + p.start
        return _dot(xn_ref[...], _panel(win_refs, c)) + bin_ref[:, c:c + MXU_WIDTH]

    def branch_proj(a16_ref_, i, p):
        return _dot(a16_ref_[...], _panel(wbr_refs, i * d + p.start))

    lg = lbl_ref[...]
    ex = jnp.exp(lg - jnp.max(lg, axis=0, keepdims=True))
    sm = ex / jnp.sum(ex, axis=0, keepdims=True)
    lb = jnp.zeros((1, d), F32)
    for i in range(1, layer + 1):
        lb = lb + sm[i:i + 1, :]

    for p in panels:
        lxe_ref[lru_hist:lru_hist + ts, p] = proj(4, p)
    xb_ref = y_ref
    conv_rows = ts // 2
    lru_conv = _conv_unit(lxe_ref, lru_hist, lcw_ref, lcb_ref, xb_ref, row_block=conv_rows, lane_block=LANES)
    cv_conv = _conv_unit(cue_ref, cv_hist, cw_ref, cb_ref, cv_ref, row_block=conv_rows, lane_block=LANES)

    def lru_conv_task(l0):
        lru_conv(0, slice(l0, l0 + LANES))
        lru_conv(conv_rows, slice(l0, l0 + LANES))

    def glu_task(p):
        cue_ref[cv_hist:cv_hist + ts, p] = proj(6, p) * _sigmoid(proj(7, p))

    @stage
    def _():
        _emit_interleaved([functools.partial(lru_conv_task, l0) for l0 in range(0, d, LANES)],
                          [functools.partial(glu_task, p) for p in panels])
        lxe_ref[0:lru_hist, :] = lxe_ref[ts:ts + lru_hist, :]

    def q_task(p):
        q_ref[:, p] = _silu(proj(0, p))

    def f_task(p):
        f_ref[:, p] = lb[:, p] + (1.0 - lb[:, p]) * _sigmoid(proj(1, p))

    def v_task(p):
        v_ref[:, p] = proj(2, p)

    def lgate_task(p):
        m_ref[:, p] = _gelu_tanh(proj(5, p))

    lam = -lam_ref[...]
    softplus = jnp.maximum(lam, 0.0) + jnp.log(1.0 + jnp.exp(-jnp.abs(lam)))

    def lru_task(h):
        cols = slice(h * HEAD_DIM, (h + 1) * HEAD_DIM)
        xb = xb_ref[:, cols]
        gates = _dot(xb.astype(BF16), lgw_ref[h])
        gr = gates[:, :HEAD_DIM] + lgb_ref[0:1, cols]
        gi = gates[:, HEAD_DIM:] + lgb_ref[1:2, cols]
        a_t = jnp.exp((-LRU_C) * _sigmoid(gr) * softplus[:, cols])
        u_t = jnp.sqrt(1.0 - a_t * a_t) * (_sigmoid(gi) * xb)
        hh, last = _lru_scan(a_t, u_t, hl_ref[:, cols])
        hl_ref[:, cols] = last
        a16_ref[:, cols] = (hh * m_ref[:, cols]).astype(BF16)

    vector_tasks = [functools.partial(cv_conv, 0, slice(l0, l0 + LANES)) for l0 in range(0, d, LANES)]
    vector_tasks += [functools.partial(lru_task, h) for h in range(d // HEAD_DIM)]
    @stage
    def _():
        _emit_interleaved(
            vector_tasks, [functools.partial(t, p) for t in (lgate_task, q_task, f_task, v_task) for p in panels])

    def cumsum_chunk(c):
        tri = (lax.broadcasted_iota(jnp.int32, (chunk, chunk), 0)
               >= lax.broadcasted_iota(jnp.int32, (chunk, chunk), 1)).astype(BF16)
        rows = slice(c * chunk, (c + 1) * chunk)
        g_hi, g_mid, g_lo = _split3(jnp.log2(f_ref[rows, :]))
        b_ref[rows, :] = _dot(tri, g_hi) + _dot(tri, g_mid) + _dot(tri, g_lo)


    def hgrn2_chunk(c):
        rows = slice(c * chunk, (c + 1) * chunk)
        masks = _level_masks(chunk)
        heads = [slice(h * HEAD_DIM, (h + 1) * HEAD_DIM) for h in range(n_heads)]
        scores, carried = [], []
        for h, cols in enumerate(heads):
            q, f, v, b = q_ref[rows, cols], f_ref[rows, cols], v_ref[rows, cols], b_ref[rows, cols]
            k = 1.0 - f
            scores.append(_hgrn2_scores(q, k, f, b))
            st = st_ref[h]
            carried.append(_dot_nt((q * jnp.exp2(b)).astype(BF16), st.astype(BF16)))
            b_last = b[chunk - 1:chunk, :]
            kd = (k * jnp.exp2(b_last - b)).astype(BF16)
            st_ref[h] = st * jnp.exp2(b_last) + _dot_tn(v.astype(BF16), kd)
        for h, cols in enumerate(heads):
            q, f, v = q_ref[rows, cols], f_ref[rows, cols], v_ref[rows, cols]
            a = jnp.zeros((chunk, chunk), F32)
            for m, al in zip(masks, scores[h]):
                a = jnp.where(m, al, a)
            o = _dot(a.astype(BF16), v.astype(BF16)) + carried[h]
            o = o + jnp.sum(q * (1.0 - f), axis=-1, keepdims=True) * v
            y_ref[rows, cols] = _rmsnorm(o, hgn_ref[...])

    @stage
    def _():
        for c in range(ts // chunk):
            cumsum_chunk(c)
        for c in range(ts // chunk):
            hgrn2_chunk(c)

    ga_ref, gb_ref, gc_ref = q_ref, f_ref, v_ref

    def yhg_task(p):
        y16_ref[:, p] = (y_ref[:, p] * _silu(proj(3, p))).astype(BF16)

    def gate_task(ref, i, p):
        ref[:, p] = _sigmoid(proj(i, p))

    def merge_ab_task(p):
        m_ref[:, p] = (ga_ref[:, p] * branch_proj(y16_ref, 0, p)
                       + gb_ref[:, p] * branch_proj(a16_ref, 1, p))

    matmul_tasks = [functools.partial(yhg_task, p) for p in panels]
    matmul_tasks += [functools.partial(gate_task, ga_ref, 8, p) for p in panels]
    matmul_tasks += [functools.partial(gate_task, gb_ref, 9, p) for p in panels]
    matmul_tasks += [functools.partial(merge_ab_task, p) for p in panels]
    matmul_tasks += [functools.partial(gate_task, gc_ref, 10, p) for p in panels]

    @stage
    def _():
        _emit_interleaved(
            [functools.partial(cv_conv, conv_rows, slice(l0, l0 + LANES)) for l0 in range(0, d, LANES)],
            matmul_tasks)
        cue_ref[0:cv_hist, :] = cue_ref[ts:ts + cv_hist, :]

    def ln_task(r0):
        rows = slice(r0, r0 + conv_rows)
        cv = cv_ref[rows, :]
        mu = jnp.mean(cv, axis=-1, keepdims=True)
        xc = cv - mu
        ln = xc * lax.rsqrt(jnp.mean(xc * xc, axis=-1, keepdims=True) + LN_EPS) * lng_ref[...] + lnb_ref[...]
        xn_ref[rows, :] = _silu(ln).astype(BF16)

    @stage
    def _():
        for r0 in range(0, ts, conv_rows):
            ln_task(r0)
        for p in panels:
            y16_ref[:, p] = (m_ref[:, p] + gc_ref[:, p] * branch_proj(xn_ref, 2, p)).astype(BF16)
        for p in panels:
            o_ref[:, p] = x_ref[:, p] + _dot(y16_ref[...], _panel(wout_refs, p.start))


def _mixer(x, layer, params, *, ts):
    b, s, d = x.shape
    n_heads = d // HEAD_DIM
    pairs = [p if isinstance(p, tuple) else (p, _layer_resident(p.shape, layer)) for p in params]
    args = [x] + [a for a, _ in pairs]
    in_specs = [pl.BlockSpec((None, ts, d), lambda bi, si: (bi, si, 0))] + [spec for _, spec in pairs]
    return pl.pallas_call(
        functools.partial(_mixer_kernel, layer=layer, n_heads=n_heads),
        out_shape=jax.ShapeDtypeStruct((b, s, d), F32),
        grid=(b, s // ts),
        in_specs=in_specs,
        out_specs=pl.BlockSpec((None, ts, d), lambda bi, si: (bi, si, 0)),
        scratch_shapes=[
            pltpu.VMEM((n_heads, HEAD_DIM, HEAD_DIM), F32),
            pltpu.VMEM((1, d), F32),
            pltpu.VMEM((ts + SUBLANES, d), F32),
            pltpu.VMEM((ts + 4 * SUBLANES, d), F32),
            pltpu.VMEM((ts, d), BF16),
            pltpu.VMEM((ts, d), BF16),
            pltpu.VMEM((ts, d), BF16),
            pltpu.VMEM((ts, d), F32),
            pltpu.VMEM((ts, d), F32),
            pltpu.VMEM((ts, d), F32),
            pltpu.VMEM((ts, d), F32),
            pltpu.VMEM((ts, d), F32),
            pltpu.VMEM((ts, d), F32),
            pltpu.VMEM((ts, d), F32),
        ],
        compiler_params=pltpu.CompilerParams(
            dimension_semantics=("parallel", "arbitrary"), vmem_limit_bytes=VMEM_LIMIT),
        name="mixer",
    )(*args)


def _tile(n, want):
    t = min(n, want)
    assert n % t == 0, (n, t)
    return t


def kernel(x, norm_ffn1, ffn1_w_in, ffn1_w_out, norm_mix, w_in_mix, b_in_mix, hgrn_lb_logits, hg_norm,
           lru_conv_w, lru_conv_b, lru_gate_w, lru_gate_b, lru_lambda, cv_dw_w, cv_dw_b, cv_ln_g, cv_ln_b,
           w_branch, w_out_mix, norm_ffn2, ffn2_w_in, ffn2_w_out, norm_final):
    bsz, seq, d = x.shape
    depth = norm_ffn1.shape[0]
    t = bsz * seq
    tm = _tile(t, 512)
    ts = _tile(seq, 256)
    assert ts % HG_CHUNK == 0 and d % HEAD_DIM == 0 and d % MXU_WIDTH == 0
    d_ff = ffn1_w_out.shape[1]
    ff_chunk = 256 if d_ff % 256 == 0 else d_ff

    rows = lambda a: a[:, None, :]
    ffn1 = (rows(norm_ffn1), ffn1_w_in, ffn1_w_out)
    ffn2 = (rows(norm_ffn2), ffn2_w_in, ffn2_w_out)
    lgw = jnp.concatenate([lru_gate_w[:, 0], lru_gate_w[:, 1]], axis=-1).astype(BF16)
    taps8 = lambda w: jnp.broadcast_to(w[:, :, None, :], w.shape[:2] + (SUBLANES, w.shape[2]))
    w_in16, w_br16, w_out16 = w_in_mix.astype(BF16), w_branch.astype(BF16), w_out_mix.astype(BF16)

    def mixer_params(l):
        branch_halves = [h for i in range(w_br16.shape[1]) for h in _weight_halves(w_br16, (l, i))]
        return [rows(norm_mix), *_weight_halves(w_in16, (l,)), rows(b_in_mix),
                (hgrn_lb_logits, _resident(hgrn_lb_logits.shape)), rows(hg_norm),
                taps8(lru_conv_w), rows(lru_conv_b), lgw, lru_gate_b, rows(lru_lambda), taps8(cv_dw_w),
                rows(cv_dw_b), rows(cv_ln_g), rows(cv_ln_b), *branch_halves, *_weight_halves(w_out16, (l,))]

    g_final = norm_final.reshape(1, d)
    for l in range(depth):
        x = _ffn(x.reshape(t, d), l, *ffn1, g_final, tm=tm, ff_chunk=ff_chunk, final_norm=False)
        x = _mixer(x.reshape(bsz, seq, d), l, mixer_params(l), ts=ts)
        x = _ffn(x.reshape(t, d), l, *ffn2, g_final, tm=tm, ff_chunk=ff_chunk, final_norm=(l == depth - 1))
    return x.reshape(bsz, seq, d)
```

```python
import functools

import jax
import jax.numpy as jnp
from jax import lax
from jax.experimental import pallas as pl
from jax.experimental.pallas import tpu as pltpu

F32 = jnp.float32
BF16 = jnp.bfloat16

SUBLANES = 8
LANES = 128
CONV_LANES = 2 * LANES
MXU_WIDTH = 256
LOG2E = 1.4426950408889634
HEAD_DIM = 128
HG_CHUNK = 32
LRU_C = 8.0
RMS_EPS = 1e-6
LN_EPS = 1e-5
V7X_VMEM_BYTES = 64 * 1024 * 1024
VMEM_LIMIT = V7X_VMEM_BYTES - 8 * 1024 * 1024


def _dot(a, b):
    return jnp.dot(a, b, preferred_element_type=F32)


def _dot_nt(a, b):
    return lax.dot_general(a, b, (((1,), (1,)), ((), ())), preferred_element_type=F32)


def _dot_tn(a, b):
    return lax.dot_general(a, b, (((0,), (0,)), ((), ())), preferred_element_type=F32)


def _panel(w_refs, c):
    width = w_refs[0].shape[1]
    assert width % MXU_WIDTH == 0
    return w_refs[c // width][:, c % width:c % width + MXU_WIDTH]


def _sigmoid(x):
    return 1.0 / (1.0 + jnp.exp2(x * (-LOG2E)))


def _silu(x):
    return x * _sigmoid(x)


def _gelu_tanh(x):
    return 0.5 * x * (1.0 + jnp.tanh(0.7978845608028654 * (x + 0.044715 * (x * x * x))))


def _rmsnorm(x, g):
    return x * lax.rsqrt(jnp.mean(x * x, axis=-1, keepdims=True) + RMS_EPS) * g


def _resident(shape):
    nd = len(shape)
    return pl.BlockSpec(shape, lambda *_: (0,) * nd, pipeline_mode=pl.Buffered(1))


def _layer_resident(shape, layer):
    nd = len(shape)
    return pl.BlockSpec((None,) + tuple(shape[1:]), lambda *_: (layer,) + (0,) * (nd - 1),
                        pipeline_mode=pl.Buffered(1))


def _weight_halves(w, lead):
    k, n = w.shape[-2:]
    half = n // 2
    assert 2 * half == n and half % MXU_WIDTH == 0 and half % 1024 != 0
    block = (None,) * len(lead) + (k, half)
    return [(w, pl.BlockSpec(block, functools.partial(lambda part, *_: tuple(lead) + (0, part), part),
                             pipeline_mode=pl.Buffered(1)))
            for part in range(2)]


def _emit_interleaved(vector_tasks, matmul_tasks):
    n, m = len(vector_tasks), len(matmul_tasks)
    i = j = 0
    while i < n or j < m:
        if j >= m or (i < n and i * m <= j * n):
            vector_tasks[i]()
            i += 1
        else:
            matmul_tasks[j]()
            j += 1


def _ffn_kernel(x_ref, g_ref, win_ref, wout_ref, gfin_ref, o_ref, *, d_ff, ff_chunk, final_norm):
    x = x_ref[...]
    xn = _rmsnorm(x, g_ref[...]).astype(BF16)
    acc = jnp.zeros(x.shape, F32)
    for j in range(d_ff // ff_chunk):
        lo = j * ff_chunk
        gate = _dot(xn, win_ref[:, lo:lo + ff_chunk].astype(BF16))
        up = _dot(xn, win_ref[:, d_ff + lo:d_ff + lo + ff_chunk].astype(BF16))
        h = (_silu(gate) * up).astype(BF16)
        acc = acc + _dot(h, wout_ref[lo:lo + ff_chunk, :].astype(BF16))
    out = x + 0.5 * acc
    o_ref[...] = _rmsnorm(out, gfin_ref[...]) if final_norm else out


def _ffn(x2d, layer, g, w_in, w_out, g_final, *, tm, ff_chunk, final_norm):
    t, d = x2d.shape
    d_ff = w_out.shape[1]
    return pl.pallas_call(
        functools.partial(_ffn_kernel, d_ff=d_ff, ff_chunk=ff_chunk, final_norm=final_norm),
        out_shape=jax.ShapeDtypeStruct((t, d), F32),
        grid=(t // tm,),
        in_specs=[
            pl.BlockSpec((tm, d), lambda i: (i, 0)),
            _layer_resident(g.shape, layer),
            _layer_resident(w_in.shape, layer),
            _layer_resident(w_out.shape, layer),
            _resident(g_final.shape),
        ],
        out_specs=pl.BlockSpec((tm, d), lambda i: (i, 0)),
        compiler_params=pltpu.CompilerParams(
            dimension_semantics=("parallel",), vmem_limit_bytes=VMEM_LIMIT),
        name="ffn",
    )(x2d, g, w_in, w_out, g_final)


def _split3(x):
    hi = x.astype(BF16)
    r = x - hi.astype(F32)
    mid = r.astype(BF16)
    lo = (r - mid.astype(F32)).astype(BF16)
    return hi, mid, lo


def _block_mid(b, hs):
    c = b.shape[0]
    blk = 2 * hs

    def rows_every8(off):
        return jnp.concatenate(
            [jnp.broadcast_to(b[r + off:r + off + 1, :], (SUBLANES, b.shape[1]))
             for r in range(0, c, SUBLANES)], axis=0)

    if blk >= SUBLANES:
        return jnp.concatenate(
            [jnp.broadcast_to(b[r + hs - 1:r + hs, :], (blk, b.shape[1])) for r in range(0, c, blk)], axis=0)
    assert blk == 4
    row8 = lax.broadcasted_iota(jnp.int32, b.shape, 0) % SUBLANES
    return jnp.where(row8 < 4, rows_every8(1), rows_every8(5))


def _level_masks(c):
    ri = lax.broadcasted_iota(jnp.int32, (c, c), 0)
    ci = lax.broadcasted_iota(jnp.int32, (c, c), 1)
    masks = []
    hs = c // 2
    while hs >= 1:
        blk = 2 * hs
        masks.append(((ri // blk) == (ci // blk)) & ((ri % blk) >= hs) & ((ci % blk) < hs))
        hs //= 2
    return masks


def _hgrn2_scores(q, k, f, b):
    c = q.shape[0]
    odd = (lax.broadcasted_iota(jnp.int32, q.shape, 0) % 2) == 1
    out = []
    hs = c // 2
    while hs >= 1:
        e = jnp.where(odd, f, 1.0) if hs == 1 else jnp.exp2(-jnp.abs(b - _block_mid(b, hs)))
        out.append(_dot_nt((q * e).astype(BF16), (k * e).astype(BF16)))
        hs //= 2
    return out


def _conv_unit(ext_ref, hist, w_ref, bias_ref, out_ref, *, row_block, lane_block):
    n_tap = w_ref.shape[0]
    base = hist - (n_tap - 1)
    groups = row_block // SUBLANES

    def unit(c0, lanes):
        sub = lax.broadcasted_iota(jnp.int32, (groups, SUBLANES, lane_block), 1)
        acc = jnp.broadcast_to(bias_ref[:, lanes][None], (groups, SUBLANES, lane_block))
        for r in range(SUBLANES):
            taps = [i for i in range(n_tap) if (base + i) % SUBLANES == r]
            if not taps:
                continue
            rows = row_block if r == 0 else row_block + SUBLANES
            p = None
            for i in taps:
                a0 = c0 + base + i - r
                win = ext_ref[a0:a0 + rows, lanes].reshape(rows // SUBLANES, SUBLANES, lane_block)
                term = win * w_ref[i, :, lanes][None]
                p = term if p is None else p + term
            if r:
                pr = pltpu.roll(p, SUBLANES - r, 1)
                p = jnp.where(sub < SUBLANES - r, pr[0:groups], pr[1:groups + 1])
            acc = acc + p
        out_ref[c0:c0 + row_block, lanes] = acc.reshape(row_block, lane_block)

    return unit


def _lru_scan(a_t, u_t, carry):
    ts, n = a_t.shape
    groups = ts // SUBLANES
    a3 = a_t.reshape(groups, SUBLANES, n)
    h3 = u_t.reshape(groups, SUBLANES, n)
    row = lax.broadcasted_iota(jnp.int32, a3.shape, 1)
    sh = 1
    while sh < SUBLANES:
        valid = row >= sh
        h_prev = jnp.where(valid, pltpu.roll(h3, sh, 1), 0.0)
        a_prev = jnp.where(valid, pltpu.roll(a3, sh, 1), 1.0)
        h3 = h3 + a3 * h_prev
        a3 = a3 * a_prev
        sh *= 2
    out = []
    for g in range(groups):
        hg = h3[g] + a3[g] * carry
        out.append(hg)
        carry = hg[SUBLANES - 1:SUBLANES, :]
    return jnp.concatenate(out, axis=0), carry


def _mixer_kernel(x_ref, nrm_ref, win0_ref, win1_ref, bin_ref, lbl_ref, hgn_ref, lcw_ref, lcb_ref, lgw_ref,
                  lgb_ref, lam_ref, cw_ref, cb_ref, lng_ref, lnb_ref,
                  wbr0_ref, wbr1_ref, wbr2_ref, wbr3_ref, wbr4_ref, wbr5_ref, wout0_ref, wout1_ref,
                  o_ref,
                  st_ref, hl_ref, lxe_ref, cue_ref, xn_ref, y16_ref, a16_ref,
                  q_ref, f_ref, v_ref, b_ref, y_ref, m_ref, cv_ref,
                  *, layer, n_heads):
    ts, d = x_ref.shape
    chunk = HG_CHUNK
    lru_hist, cv_hist = SUBLANES, 4 * SUBLANES
    panels = [slice(c, c + MXU_WIDTH) for c in range(0, d, MXU_WIDTH)]
    win_refs, wout_refs = (win0_ref, win1_ref), (wout0_ref, wout1_ref)
    wbr_refs = (wbr0_ref, wbr1_ref, wbr2_ref, wbr3_ref, wbr4_ref, wbr5_ref)

    @pl.when(pl.program_id(1) == 0)
    def _():
        st_ref[...] = jnp.zeros(st_ref.shape, F32)
        hl_ref[...] = jnp.zeros(hl_ref.shape, F32)
        lxe_ref[0:lru_hist, :] = jnp.zeros((lru_hist, d), F32)
        cue_ref[0:cv_hist, :] = jnp.zeros((cv_hist, d), F32)

    def stage(body):
        body()

    xn_ref[...] = _rmsnorm(x_ref[...], nrm_ref[...]).astype(BF16)

    def proj(i, p):
        c = i * d + p.start
        return _dot(xn_ref[...], _panel(win_refs, c)) + bin_ref[:, c:c + MXU_WIDTH]

    def branch_proj(a16_ref_, i, p):
        return _dot(a16_ref_[...], _panel(wbr_refs, i * d + p.start))

    lg = lbl_ref[...]
    ex = jnp.exp(lg - jnp.max(lg, axis=0, keepdims=True))
    sm = ex / jnp.sum(ex, axis=0, keepdims=True)
    lb = jnp.zeros((1, d), F32)
    for i in range(1, layer + 1):
        lb = lb + sm[i:i + 1, :]

    for p in panels:
        lxe_ref[lru_hist:lru_hist + ts, p] = proj(4, p)
    xb_ref = y_ref
    conv_rows = ts // 2
    lru_conv = _conv_unit(lxe_ref, lru_hist, lcw_ref, lcb_ref, xb_ref, row_block=conv_rows, lane_block=CONV_LANES)
    cv_conv = _conv_unit(cue_ref, cv_hist, cw_ref, cb_ref, cv_ref, row_block=conv_rows, lane_block=CONV_LANES)
    conv_cols = [slice(l0, l0 + CONV_LANES) for l0 in range(0, d, CONV_LANES)]

    def lru_conv_task(lanes):
        lru_conv(0, lanes)
        lru_conv(conv_rows, lanes)

    def glu_task(p):
        cue_ref[cv_hist:cv_hist + ts, p] = proj(6, p) * _sigmoid(proj(7, p))

    @stage
    def _():
        _emit_interleaved([functools.partial(lru_conv_task, lanes) for lanes in conv_cols],
                          [functools.partial(glu_task, p) for p in panels])
        lxe_ref[0:lru_hist, :] = lxe_ref[ts:ts + lru_hist, :]

    def q_task(p):
        q_ref[:, p] = _silu(proj(0, p))

    def f_task(p):
        f_ref[:, p] = lb[:, p] + (1.0 - lb[:, p]) * _sigmoid(proj(1, p))

    def v_task(p):
        v_ref[:, p] = proj(2, p)

    def lgate_task(p):
        m_ref[:, p] = _gelu_tanh(proj(5, p))

    lam = -lam_ref[...]
    softplus = jnp.maximum(lam, 0.0) + jnp.log(1.0 + jnp.exp(-jnp.abs(lam)))

    def lru_task(h):
        cols = slice(h * HEAD_DIM, (h + 1) * HEAD_DIM)
        xb = xb_ref[:, cols]
        gates = _dot(xb.astype(BF16), lgw_ref[h])
        gr = gates[:, :HEAD_DIM] + lgb_ref[0:1, cols]
        gi = gates[:, HEAD_DIM:] + lgb_ref[1:2, cols]
        a_t = jnp.exp((-LRU_C) * _sigmoid(gr) * softplus[:, cols])
        u_t = jnp.sqrt(1.0 - a_t * a_t) * (_sigmoid(gi) * xb)
        hh, last = _lru_scan(a_t, u_t, hl_ref[:, cols])
        hl_ref[:, cols] = last
        a16_ref[:, cols] = (hh * m_ref[:, cols]).astype(BF16)

    vector_tasks = [functools.partial(cv_conv, 0, lanes) for lanes in conv_cols]
    vector_tasks += [functools.partial(lru_task, h) for h in range(d // HEAD_DIM)]
    @stage
    def _():
        _emit_interleaved(
            vector_tasks, [functools.partial(t, p) for t in (lgate_task, q_task, f_task, v_task) for p in panels])

    def cumsum_chunk(c):
        tri = (lax.broadcasted_iota(jnp.int32, (chunk, chunk), 0)
               >= lax.broadcasted_iota(jnp.int32, (chunk, chunk), 1)).astype(BF16)
        rows = slice(c * chunk, (c + 1) * chunk)
        g_hi, g_mid, g_lo = _split3(jnp.log2(f_ref[rows, :]))
        b_ref[rows, :] = _dot(tri, g_hi) + _dot(tri, g_mid) + _dot(tri, g_lo)


    def hgrn2_chunk(c):
        rows = slice(c * chunk, (c + 1) * chunk)
        masks = _level_masks(chunk)
        heads = [slice(h * HEAD_DIM, (h + 1) * HEAD_DIM) for h in range(n_heads)]
        scores, carried = [], []
        for h, cols in enumerate(heads):
            q, f, v, b = q_ref[rows, cols], f_ref[rows, cols], v_ref[rows, cols], b_ref[rows, cols]
            k = 1.0 - f
            scores.append(_hgrn2_scores(q, k, f, b))
            st = st_ref[h]
            carried.append(_dot_nt((q * jnp.exp2(b)).astype(BF16), st.astype(BF16)))
            b_last = b[chunk - 1:chunk, :]
            kd = (k * jnp.exp2(b_last - b)).astype(BF16)
            st_ref[h] = st * jnp.exp2(b_last) + _dot_tn(v.astype(BF16), kd)
        for h, cols in enumerate(heads):
            q, f, v = q_ref[rows, cols], f_ref[rows, cols], v_ref[rows, cols]
            a = jnp.zeros((chunk, chunk), F32)
            for m, al in zip(masks, scores[h]):
                a = jnp.where(m, al, a)
            o = _dot(a.astype(BF16), v.astype(BF16)) + carried[h]
            o = o + jnp.sum(q * (1.0 - f), axis=-1, keepdims=True) * v
            y_ref[rows, cols] = _rmsnorm(o, hgn_ref[...])

    @stage
    def _():
        for c in range(ts // chunk):
            cumsum_chunk(c)
        for c in range(ts // chunk):
            hgrn2_chunk(c)

    ga_ref, gb_ref, gc_ref = q_ref, f_ref, v_ref

    def yhg_task(p):
        y16_ref[:, p] = (y_ref[:, p] * _silu(proj(3, p))).astype(BF16)

    def gate_task(ref, i, p):
        ref[:, p] = _sigmoid(proj(i, p))

    def merge_ab_task(p):
        m_ref[:, p] = (ga_ref[:, p] * branch_proj(y16_ref, 0, p)
                       + gb_ref[:, p] * branch_proj(a16_ref, 1, p))

    matmul_tasks = [functools.partial(yhg_task, p) for p in panels]
    matmul_tasks += [functools.partial(gate_task, ga_ref, 8, p) for p in panels]
    matmul_tasks += [functools.partial(gate_task, gb_ref, 9, p) for p in panels]
    matmul_tasks += [functools.partial(merge_ab_task, p) for p in panels]
    matmul_tasks += [functools.partial(gate_task, gc_ref, 10, p) for p in panels]

    @stage
    def _():
        _emit_interleaved(
            [functools.partial(cv_conv, conv_rows, lanes) for lanes in conv_cols],
            matmul_tasks)
        cue_ref[0:cv_hist, :] = cue_ref[ts:ts + cv_hist, :]

    def ln_task(r0):
        rows = slice(r0, r0 + conv_rows)
        cv = cv_ref[rows, :]
        mu = jnp.mean(cv, axis=-1, keepdims=True)
        xc = cv - mu
        ln = xc * lax.rsqrt(jnp.mean(xc * xc, axis=-1, keepdims=True) + LN_EPS) * lng_ref[...] + lnb_ref[...]
        xn_ref[rows, :] = _silu(ln).astype(BF16)

    @stage
    def _():
        for r0 in range(0, ts, conv_rows):
            ln_task(r0)
        for p in panels:
            y16_ref[:, p] = (m_ref[:, p] + gc_ref[:, p] * branch_proj(xn_ref, 2, p)).astype(BF16)
        for p in panels:
            o_ref[:, p] = x_ref[:, p] + _dot(y16_ref[...], _panel(wout_refs, p.start))


def _mixer(x, layer, params, *, ts):
    b, s, d = x.shape
    n_heads = d // HEAD_DIM
    pairs = [p if isinstance(p, tuple) else (p, _layer_resident(p.shape, layer)) for p in params]
    args = [x] + [a for a, _ in pairs]
    in_specs = [pl.BlockSpec((None, ts, d), lambda bi, si: (bi, si, 0))] + [spec for _, spec in pairs]
    return pl.pallas_call(
        functools.partial(_mixer_kernel, layer=layer, n_heads=n_heads),
        out_shape=jax.ShapeDtypeStruct((b, s, d), F32),
        grid=(b, s // ts),
        in_specs=in_specs,
        out_specs=pl.BlockSpec((None, ts, d), lambda bi, si: (bi, si, 0)),
        scratch_shapes=[
            pltpu.VMEM((n_heads, HEAD_DIM, HEAD_DIM), F32),
            pltpu.VMEM((1, d), F32),
            pltpu.VMEM((ts + SUBLANES, d), F32),
            pltpu.VMEM((ts + 4 * SUBLANES, d), F32),
            pltpu.VMEM((ts, d), BF16),
            pltpu.VMEM((ts, d), BF16),
            pltpu.VMEM((ts, d), BF16),
            pltpu.VMEM((ts, d), F32),
            pltpu.VMEM((ts, d), F32),
            pltpu.VMEM((ts, d), F32),
            pltpu.VMEM((ts, d), F32),
            pltpu.VMEM((ts, d), F32),
            pltpu.VMEM((ts, d), F32),
            pltpu.VMEM((ts, d), F32),
        ],
        compiler_params=pltpu.CompilerParams(
            dimension_semantics=("parallel", "arbitrary"), vmem_limit_bytes=VMEM_LIMIT),
        name="mixer",
    )(*args)


def _tile(n, want):
    t = min(n, want)
    assert n % t == 0, (n, t)
    return t


def kernel(x, norm_ffn1, ffn1_w_in, ffn1_w_out, norm_mix, w_in_mix, b_in_mix, hgrn_lb_logits, hg_norm,
           lru_conv_w, lru_conv_b, lru_gate_w, lru_gate_b, lru_lambda, cv_dw_w, cv_dw_b, cv_ln_g, cv_ln_b,
           w_branch, w_out_mix, norm_ffn2, ffn2_w_in, ffn2_w_out, norm_final):
    bsz, seq, d = x.shape
    depth = norm_ffn1.shape[0]
    t = bsz * seq
    tm = _tile(t, 512)
    ts = _tile(seq, 256)
    assert ts % HG_CHUNK == 0 and d % HEAD_DIM == 0 and d % MXU_WIDTH == 0
    d_ff = ffn1_w_out.shape[1]
    ff_chunk = 256 if d_ff % 256 == 0 else d_ff

    rows = lambda a: a[:, None, :]
    ffn1 = (rows(norm_ffn1), ffn1_w_in, ffn1_w_out)
    ffn2 = (rows(norm_ffn2), ffn2_w_in, ffn2_w_out)
    lgw = jnp.concatenate([lru_gate_w[:, 0], lru_gate_w[:, 1]], axis=-1).astype(BF16)
    taps8 = lambda w: jnp.broadcast_to(w[:, :, None, :], w.shape[:2] + (SUBLANES, w.shape[2]))
    w_in16, w_br16, w_out16 = w_in_mix.astype(BF16), w_branch.astype(BF16), w_out_mix.astype(BF16)

    def mixer_params(l):
        branch_halves = [h for i in range(w_br16.shape[1]) for h in _weight_halves(w_br16, (l, i))]
        return [rows(norm_mix), *_weight_halves(w_in16, (l,)), rows(b_in_mix),
                (hgrn_lb_logits, _resident(hgrn_lb_logits.shape)), rows(hg_norm),
                taps8(lru_conv_w), rows(lru_conv_b), lgw, lru_gate_b, rows(lru_lambda), taps8(cv_dw_w),
                rows(cv_dw_b), rows(cv_ln_g), rows(cv_ln_b), *branch_halves, *_weight_halves(w_out16, (l,))]

    g_final = norm_final.reshape(1, d)
    for l in range(depth):
        x = _ffn(x.reshape(t, d), l, *ffn1, g_final, tm=tm, ff_chunk=ff_chunk, final_norm=False)
        x = _mixer(x.reshape(bsz, seq, d), l, mixer_params(l), ts=ts)
        x = _ffn(x.reshape(t, d), l, *ffn2, g_final, tm=tm, ff_chunk=ff_chunk, final_norm=(l == depth - 1))
    return x.reshape(bsz, seq, d)
```

```python
import functools

import jax
import jax.numpy as jnp
from jax import lax
from jax.experimental import pallas as pl
from jax.experimental.pallas import tpu as pltpu

F32 = jnp.float32
BF16 = jnp.bfloat16

SUBLANES = 8
LANES = 128
CONV_LANES = 2 * LANES
MXU_WIDTH = 256
LOG2E = 1.4426950408889634
HEAD_DIM = 128
HG_CHUNK = 32
LRU_C = 8.0
RMS_EPS = 1e-6
LN_EPS = 1e-5
V7X_VMEM_BYTES = 64 * 1024 * 1024
VMEM_LIMIT = V7X_VMEM_BYTES - 8 * 1024 * 1024


def _dot(a, b):
    return jnp.dot(a, b, preferred_element_type=F32)


def _dot_nt(a, b):
    return lax.dot_general(a, b, (((1,), (1,)), ((), ())), preferred_element_type=F32)


def _dot_tn(a, b):
    return lax.dot_general(a, b, (((0,), (0,)), ((), ())), preferred_element_type=F32)


def _panel(w_refs, c):
    width = w_refs[0].shape[1]
    assert width % MXU_WIDTH == 0
    return w_refs[c // width][:, c % width:c % width + MXU_WIDTH]


def _sigmoid(x):
    return 1.0 / (1.0 + jnp.exp2(x * (-LOG2E)))


def _silu(x):
    return x * _sigmoid(x)


def _gelu_tanh(x):
    return 0.5 * x * (1.0 + jnp.tanh(0.7978845608028654 * (x + 0.044715 * (x * x * x))))


def _rmsnorm(x, g):
    return x * lax.rsqrt(jnp.mean(x * x, axis=-1, keepdims=True) + RMS_EPS) * g


def _resident(shape):
    nd = len(shape)
    return pl.BlockSpec(shape, lambda *_: (0,) * nd, pipeline_mode=pl.Buffered(1))


def _layer_resident(shape, layer):
    nd = len(shape)
    return pl.BlockSpec((None,) + tuple(shape[1:]), lambda *_: (layer,) + (0,) * (nd - 1),
                        pipeline_mode=pl.Buffered(1))


def _weight_halves(w, lead):
    k, n = w.shape[-2:]
    half = n // 2
    assert 2 * half == n and half % MXU_WIDTH == 0 and half % 1024 != 0
    block = (None,) * len(lead) + (k, half)
    return [(w, pl.BlockSpec(block, functools.partial(lambda part, *_: tuple(lead) + (0, part), part),
                             pipeline_mode=pl.Buffered(1)))
            for part in range(2)]


def _emit_interleaved(vector_tasks, matmul_tasks):
    n, m = len(vector_tasks), len(matmul_tasks)
    i = j = 0
    while i < n or j < m:
        if j >= m or (i < n and i * m <= j * n):
            vector_tasks[i]()
            i += 1
        else:
            matmul_tasks[j]()
            j += 1


def _ffn_kernel(x_ref, g_ref, win_ref, wout_ref, gfin_ref, o_ref, *, d_ff, ff_chunk, final_norm):
    x = x_ref[...]
    xn = _rmsnorm(x, g_ref[...]).astype(BF16)
    acc = jnp.zeros(x.shape, F32)
    for j in range(d_ff // ff_chunk):
        lo = j * ff_chunk
        gate = _dot(xn, win_ref[:, lo:lo + ff_chunk].astype(BF16))
        up = _dot(xn, win_ref[:, d_ff + lo:d_ff + lo + ff_chunk].astype(BF16))
        h = (_silu(gate) * up).astype(BF16)
        acc = acc + _dot(h, wout_ref[lo:lo + ff_chunk, :].astype(BF16))
    out = x + 0.5 * acc
    o_ref[...] = _rmsnorm(out, gfin_ref[...]) if final_norm else out


def _ffn(x2d, layer, g, w_in, w_out, g_final, *, tm, ff_chunk, final_norm):
    t, d = x2d.shape
    d_ff = w_out.shape[1]
    return pl.pallas_call(
        functools.partial(_ffn_kernel, d_ff=d_ff, ff_chunk=ff_chunk, final_norm=final_norm),
        out_shape=jax.ShapeDtypeStruct((t, d), F32),
        grid=(t // tm,),
        in_specs=[
            pl.BlockSpec((tm, d), lambda i: (i, 0)),
            _layer_resident(g.shape, layer),
            _layer_resident(w_in.shape, layer),
            _layer_resident(w_out.shape, layer),
            _resident(g_final.shape),
        ],
        out_specs=pl.BlockSpec((tm, d), lambda i: (i, 0)),
        compiler_params=pltpu.CompilerParams(
            dimension_semantics=("parallel",), vmem_limit_bytes=VMEM_LIMIT),
        name="ffn",
    )(x2d, g, w_in, w_out, g_final)


def _split3(x):
    hi = x.astype(BF16)
    r = x - hi.astype(F32)
    mid = r.astype(BF16)
    lo = (r - mid.astype(F32)).astype(BF16)
    return hi, mid, lo


def _block_mid(b, hs):
    c = b.shape[0]
    blk = 2 * hs

    def rows_every8(off):
        return jnp.concatenate(
            [jnp.broadcast_to(b[r + off:r + off + 1, :], (SUBLANES, b.shape[1]))
             for r in range(0, c, SUBLANES)], axis=0)

    if blk >= SUBLANES:
        return jnp.concatenate(
            [jnp.broadcast_to(b[r + hs - 1:r + hs, :], (blk, b.shape[1])) for r in range(0, c, blk)], axis=0)
    assert blk == 4
    row8 = lax.broadcasted_iota(jnp.int32, b.shape, 0) % SUBLANES
    return jnp.where(row8 < 4, rows_every8(1), rows_every8(5))


def _level_masks(c):
    ri = lax.broadcasted_iota(jnp.int32, (c, c), 0)
    ci = lax.broadcasted_iota(jnp.int32, (c, c), 1)
    masks = []
    hs = c // 2
    while hs >= 1:
        blk = 2 * hs
        masks.append(((ri // blk) == (ci // blk)) & ((ri % blk) >= hs) & ((ci % blk) < hs))
        hs //= 2
    return masks


def _hgrn2_scores(q, k, f, b):
    c = q.shape[0]
    row = lax.broadcasted_iota(jnp.int32, q.shape, 0)
    out = []
    hs = c // 2
    while hs >= 1:
        upper = (row % (2 * hs)) >= hs
        e = jnp.where(upper, f, 1.0) if hs == 1 else jnp.exp2(-jnp.abs(b - _block_mid(b, hs)))
        x = (jnp.where(upper, q, k) * e).astype(BF16)
        out.append(_dot_nt(x, x))
        hs //= 2
    return out


def _conv_unit(ext_ref, hist, w_ref, bias_ref, out_ref, *, row_block, lane_block):
    n_tap = w_ref.shape[0]
    base = hist - (n_tap - 1)
    groups = row_block // SUBLANES

    def unit(c0, lanes):
        sub = lax.broadcasted_iota(jnp.int32, (groups, SUBLANES, lane_block), 1)
        acc = jnp.broadcast_to(bias_ref[:, lanes][None], (groups, SUBLANES, lane_block))
        for r in range(SUBLANES):
            taps = [i for i in range(n_tap) if (base + i) % SUBLANES == r]
            if not taps:
                continue
            rows = row_block if r == 0 else row_block + SUBLANES
            p = None
            for i in taps:
                a0 = c0 + base + i - r
                win = ext_ref[a0:a0 + rows, lanes].reshape(rows // SUBLANES, SUBLANES, lane_block)
                term = win * w_ref[i, :, lanes][None]
                p = term if p is None else p + term
            if r:
                pr = pltpu.roll(p, SUBLANES - r, 1)
                p = jnp.where(sub < SUBLANES - r, pr[0:groups], pr[1:groups + 1])
            acc = acc + p
        out_ref[c0:c0 + row_block, lanes] = acc.reshape(row_block, lane_block)

    return unit


def _lru_scan(a_t, u_t, carry):
    ts, n = a_t.shape
    groups = ts // SUBLANES
    a3 = a_t.reshape(groups, SUBLANES, n)
    h3 = u_t.reshape(groups, SUBLANES, n)
    row = lax.broadcasted_iota(jnp.int32, a3.shape, 1)
    sh = 1
    while sh < SUBLANES:
        valid = row >= sh
        h_prev = jnp.where(valid, pltpu.roll(h3, sh, 1), 0.0)
        a_prev = jnp.where(valid, pltpu.roll(a3, sh, 1), 1.0)
        h3 = h3 + a3 * h_prev
        a3 = a3 * a_prev
        sh *= 2
    out = []
    for g in range(groups):
        hg = h3[g] + a3[g] * carry
        out.append(hg)
        carry = hg[SUBLANES - 1:SUBLANES, :]
    return jnp.concatenate(out, axis=0), carry


def _mixer_kernel(x_ref, nrm_ref, win0_ref, win1_ref, bin_ref, lbl_ref, hgn_ref, lcw_ref, lcb_ref, lgw_ref,
                  lgb_ref, lam_ref, cw_ref, cb_ref, lng_ref, lnb_ref,
                  wbr0_ref, wbr1_ref, wbr2_ref, wbr3_ref, wbr4_ref, wbr5_ref, wout0_ref, wout1_ref,
                  o_ref,
                  st_ref, hl_ref, lxe_ref, cue_ref, xn_ref, y16_ref, a16_ref,
                  q_ref, f_ref, v_ref, b_ref, y_ref, m_ref, cv_ref,
                  *, layer, n_heads):
    ts, d = x_ref.shape
    chunk = HG_CHUNK
    lru_hist, cv_hist = SUBLANES, 4 * SUBLANES
    panels = [slice(c, c + MXU_WIDTH) for c in range(0, d, MXU_WIDTH)]
    win_refs, wout_refs = (win0_ref, win1_ref), (wout0_ref, wout1_ref)
    wbr_refs = (wbr0_ref, wbr1_ref, wbr2_ref, wbr3_ref, wbr4_ref, wbr5_ref)

    @pl.when(pl.program_id(1) == 0)
    def _():
        st_ref[...] = jnp.zeros(st_ref.shape, F32)
        hl_ref[...] = jnp.zeros(hl_ref.shape, F32)
        lxe_ref[0:lru_hist, :] = jnp.zeros((lru_hist, d), F32)
        cue_ref[0:cv_hist, :] = jnp.zeros((cv_hist, d), F32)

    def stage(body):
        body()

    xn_ref[...] = _rmsnorm(x_ref[...], nrm_ref[...]).astype(BF16)

    def proj(i, p):
        c = i * d + p.start
        return _dot(xn_ref[...], _panel(win_refs, c)) + bin_ref[:, c:c + MXU_WIDTH]

    def branch_proj(a16_ref_, i, p):
        return _dot(a16_ref_[...], _panel(wbr_refs, i * d + p.start))

    lg = lbl_ref[...]
    ex = jnp.exp(lg - jnp.max(lg, axis=0, keepdims=True))
    sm = ex / jnp.sum(ex, axis=0, keepdims=True)
    lb = jnp.zeros((1, d), F32)
    for i in range(1, layer + 1):
        lb = lb + sm[i:i + 1, :]

    for p in panels:
        lxe_ref[lru_hist:lru_hist + ts, p] = proj(4, p)
    xb_ref = y_ref
    conv_rows = ts // 2
    lru_conv = _conv_unit(lxe_ref, lru_hist, lcw_ref, lcb_ref, xb_ref, row_block=conv_rows, lane_block=CONV_LANES)
    cv_conv = _conv_unit(cue_ref, cv_hist, cw_ref, cb_ref, cv_ref, row_block=conv_rows, lane_block=CONV_LANES)
    conv_cols = [slice(l0, l0 + CONV_LANES) for l0 in range(0, d, CONV_LANES)]

    def lru_conv_task(lanes):
        lru_conv(0, lanes)
        lru_conv(conv_rows, lanes)

    def glu_task(p):
        cue_ref[cv_hist:cv_hist + ts, p] = proj(6, p) * _sigmoid(proj(7, p))

    @stage
    def _():
        _emit_interleaved([functools.partial(lru_conv_task, lanes) for lanes in conv_cols],
                          [functools.partial(glu_task, p) for p in panels])
        lxe_ref[0:lru_hist, :] = lxe_ref[ts:ts + lru_hist, :]

    def q_task(p):
        q_ref[:, p] = _silu(proj(0, p))

    def f_task(p):
        f_ref[:, p] = lb[:, p] + (1.0 - lb[:, p]) * _sigmoid(proj(1, p))

    def v_task(p):
        v_ref[:, p] = proj(2, p)

    def lgate_task(p):
        m_ref[:, p] = _gelu_tanh(proj(5, p))

    lam = -lam_ref[...]
    softplus = jnp.maximum(lam, 0.0) + jnp.log(1.0 + jnp.exp(-jnp.abs(lam)))

    def lru_task(h):
        cols = slice(h * HEAD_DIM, (h + 1) * HEAD_DIM)
        xb = xb_ref[:, cols]
        gates = _dot(xb.astype(BF16), lgw_ref[h])
        gr = gates[:, :HEAD_DIM] + lgb_ref[0:1, cols]
        gi = gates[:, HEAD_DIM:] + lgb_ref[1:2, cols]
        a_t = jnp.exp((-LRU_C) * _sigmoid(gr) * softplus[:, cols])
        u_t = jnp.sqrt(1.0 - a_t * a_t) * (_sigmoid(gi) * xb)
        hh, last = _lru_scan(a_t, u_t, hl_ref[:, cols])
        hl_ref[:, cols] = last
        a16_ref[:, cols] = (hh * m_ref[:, cols]).astype(BF16)

    vector_tasks = [functools.partial(cv_conv, 0, lanes) for lanes in conv_cols]
    vector_tasks += [functools.partial(lru_task, h) for h in range(d // HEAD_DIM)]
    @stage
    def _():
        _emit_interleaved(
            vector_tasks, [functools.partial(t, p) for t in (lgate_task, q_task, f_task, v_task) for p in panels])

    def cumsum_chunk(c):
        tri = (lax.broadcasted_iota(jnp.int32, (chunk, chunk), 0)
               >= lax.broadcasted_iota(jnp.int32, (chunk, chunk), 1)).astype(BF16)
        rows = slice(c * chunk, (c + 1) * chunk)
        g_hi, g_mid, g_lo = _split3(jnp.log2(f_ref[rows, :]))
        b_ref[rows, :] = _dot(tri, g_hi) + _dot(tri, g_mid) + _dot(tri, g_lo)


    def hgrn2_chunk(c):
        rows = slice(c * chunk, (c + 1) * chunk)
        masks = _level_masks(chunk)
        heads = [slice(h * HEAD_DIM, (h + 1) * HEAD_DIM) for h in range(n_heads)]
        scores, carried = [], []
        for h, cols in enumerate(heads):
            q, f, v, b = q_ref[rows, cols], f_ref[rows, cols], v_ref[rows, cols], b_ref[rows, cols]
            k = 1.0 - f
            scores.append(_hgrn2_scores(q, k, f, b))
            st = st_ref[h]
            carried.append(_dot_nt((q * jnp.exp2(b)).astype(BF16), st.astype(BF16)))
            b_last = b[chunk - 1:chunk, :]
            kd = (k * jnp.exp2(b_last - b)).astype(BF16)
            st_ref[h] = st * jnp.exp2(b_last) + _dot_tn(v.astype(BF16), kd)
        for h, cols in enumerate(heads):
            q, f, v = q_ref[rows, cols], f_ref[rows, cols], v_ref[rows, cols]
            a = jnp.zeros((chunk, chunk), F32)
            for m, al in zip(masks, scores[h]):
                a = jnp.where(m, al, a)
            o = _dot(a.astype(BF16), v.astype(BF16)) + carried[h]
            o = o + jnp.sum(q * (1.0 - f), axis=-1, keepdims=True) * v
            y_ref[rows, cols] = _rmsnorm(o, hgn_ref[...])

    @stage
    def _():
        for c in range(ts // chunk):
            cumsum_chunk(c)
        for c in range(ts // chunk):
            hgrn2_chunk(c)

    ga_ref, gb_ref, gc_ref = q_ref, f_ref, v_ref

    def yhg_task(p):
        y16_ref[:, p] = (y_ref[:, p] * _silu(proj(3, p))).astype(BF16)

    def gate_task(ref, i, p):
        ref[:, p] = _sigmoid(proj(i, p))

    def merge_ab_task(p):
        m_ref[:, p] = (ga_ref[:, p] * branch_proj(y16_ref, 0, p)
                       + gb_ref[:, p] * branch_proj(a16_ref, 1, p))

    matmul_tasks = [functools.partial(yhg_task, p) for p in panels]
    matmul_tasks += [functools.partial(gate_task, ga_ref, 8, p) for p in panels]
    matmul_tasks += [functools.partial(gate_task, gb_ref, 9, p) for p in panels]
    matmul_tasks += [functools.partial(merge_ab_task, p) for p in panels]
    matmul_tasks += [functools.partial(gate_task, gc_ref, 10, p) for p in panels]

    @stage
    def _():
        _emit_interleaved(
            [functools.partial(cv_conv, conv_rows, lanes) for lanes in conv_cols],
            matmul_tasks)
        cue_ref[0:cv_hist, :] = cue_ref[ts:ts + cv_hist, :]

    def ln_task(r0):
        rows = slice(r0, r0 + conv_rows)
        cv = cv_ref[rows, :]
        mu = jnp.mean(cv, axis=-1, keepdims=True)
        xc = cv - mu
        ln = xc * lax.rsqrt(jnp.mean(xc * xc, axis=-1, keepdims=True) + LN_EPS) * lng_ref[...] + lnb_ref[...]
        xn_ref[rows, :] = _silu(ln).astype(BF16)

    @stage
    def _():
        for r0 in range(0, ts, conv_rows):
            ln_task(r0)
        for p in panels:
            y16_ref[:, p] = (m_ref[:, p] + gc_ref[:, p] * branch_proj(xn_ref, 2, p)).astype(BF16)
        for p in panels:
            o_ref[:, p] = x_ref[:, p] + _dot(y16_ref[...], _panel(wout_refs, p.start))


def _mixer(x, layer, params, *, ts):
    b, s, d = x.shape
    n_heads = d // HEAD_DIM
    pairs = [p if isinstance(p, tuple) else (p, _layer_resident(p.shape, layer)) for p in params]
    args = [x] + [a for a, _ in pairs]
    in_specs = [pl.BlockSpec((None, ts, d), lambda bi, si: (bi, si, 0))] + [spec for _, spec in pairs]
    return pl.pallas_call(
        functools.partial(_mixer_kernel, layer=layer, n_heads=n_heads),
        out_shape=jax.ShapeDtypeStruct((b, s, d), F32),
        grid=(b, s // ts),
        in_specs=in_specs,
        out_specs=pl.BlockSpec((None, ts, d), lambda bi, si: (bi, si, 0)),
        scratch_shapes=[
            pltpu.VMEM((n_heads, HEAD_DIM, HEAD_DIM), F32),
            pltpu.VMEM((1, d), F32),
            pltpu.VMEM((ts + SUBLANES, d), F32),
            pltpu.VMEM((ts + 4 * SUBLANES, d), F32),
            pltpu.VMEM((ts, d), BF16),
            pltpu.VMEM((ts, d), BF16),
            pltpu.VMEM((ts, d), BF16),
            pltpu.VMEM((ts, d), F32),
            pltpu.VMEM((ts, d), F32),
            pltpu.VMEM((ts, d), F32),
            pltpu.VMEM((ts, d), F32),
            pltpu.VMEM((ts, d), F32),
            pltpu.VMEM((ts, d), F32),
            pltpu.VMEM((ts, d), F32),
        ],
        compiler_params=pltpu.CompilerParams(
            dimension_semantics=("parallel", "arbitrary"), vmem_limit_bytes=VMEM_LIMIT),
        name="mixer",
    )(*args)


def _tile(n, want):
    t = min(n, want)
    assert n % t == 0, (n, t)
    return t


def kernel(x, norm_ffn1, ffn1_w_in, ffn1_w_out, norm_mix, w_in_mix, b_in_mix, hgrn_lb_logits, hg_norm,
           lru_conv_w, lru_conv_b, lru_gate_w, lru_gate_b, lru_lambda, cv_dw_w, cv_dw_b, cv_ln_g, cv_ln_b,
           w_branch, w_out_mix, norm_ffn2, ffn2_w_in, ffn2_w_out, norm_final):
    bsz, seq, d = x.shape
    depth = norm_ffn1.shape[0]
    t = bsz * seq
    tm = _tile(t, 512)
    ts = _tile(seq, 256)
    assert ts % HG_CHUNK == 0 and d % HEAD_DIM == 0 and d % MXU_WIDTH == 0
    d_ff = ffn1_w_out.shape[1]
    ff_chunk = 256 if d_ff % 256 == 0 else d_ff

    rows = lambda a: a[:, None, :]
    ffn1 = (rows(norm_ffn1), ffn1_w_in, ffn1_w_out)
    ffn2 = (rows(norm_ffn2), ffn2_w_in, ffn2_w_out)
    lgw = jnp.concatenate([lru_gate_w[:, 0], lru_gate_w[:, 1]], axis=-1).astype(BF16)
    taps8 = lambda w: jnp.broadcast_to(w[:, :, None, :], w.shape[:2] + (SUBLANES, w.shape[2]))
    w_in16, w_br16, w_out16 = w_in_mix.astype(BF16), w_branch.astype(BF16), w_out_mix.astype(BF16)

    def mixer_params(l):
        branch_halves = [h for i in range(w_br16.shape[1]) for h in _weight_halves(w_br16, (l, i))]
        return [rows(norm_mix), *_weight_halves(w_in16, (l,)), rows(b_in_mix),
                (hgrn_lb_logits, _resident(hgrn_lb_logits.shape)), rows(hg_norm),
                taps8(lru_conv_w), rows(lru_conv_b), lgw, lru_gate_b, rows(lru_lambda), taps8(cv_dw_w),
                rows(cv_dw_b), rows(cv_ln_g), rows(cv_ln_b), *branch_halves, *_weight_halves(w_out16, (l,))]

    g_final = norm_final.reshape(1, d)
    for l in range(depth):
        x = _ffn(x.reshape(t, d), l, *ffn1, g_final, tm=tm, ff_chunk=ff_chunk, final_norm=False)
        x = _mixer(x.reshape(bsz, seq, d), l, mixer_params(l), ts=ts)
        x = _ffn(x.reshape(t, d), l, *ffn2, g_final, tm=tm, ff_chunk=ff_chunk, final_norm=(l == depth - 1))
    return x.reshape(bsz, seq, d)
```

```python
import functools

import jax
import jax.numpy as jnp
from jax import lax
from jax.experimental import pallas as pl
from jax.experimental.pallas import tpu as pltpu

F32 = jnp.float32
BF16 = jnp.bfloat16

SUBLANES = 8
LANES = 128
CONV_LANES = 2 * LANES
MXU_WIDTH = 256
LOG2E = 1.4426950408889634
HEAD_DIM = 128
HG_CHUNK = 64
LRU_C = 8.0
RMS_EPS = 1e-6
LN_EPS = 1e-5
V7X_VMEM_BYTES = 64 * 1024 * 1024
VMEM_LIMIT = V7X_VMEM_BYTES - 8 * 1024 * 1024


def _dot(a, b):
    return jnp.dot(a, b, preferred_element_type=F32)


def _dot_nt(a, b):
    return lax.dot_general(a, b, (((1,), (1,)), ((), ())), preferred_element_type=F32)


def _dot_tn(a, b):
    return lax.dot_general(a, b, (((0,), (0,)), ((), ())), preferred_element_type=F32)


def _panel(w_refs, c):
    width = w_refs[0].shape[1]
    assert width % MXU_WIDTH == 0
    return w_refs[c // width][:, c % width:c % width + MXU_WIDTH]


def _sigmoid(x):
    return 1.0 / (1.0 + jnp.exp2(x * (-LOG2E)))


def _silu(x):
    return x * _sigmoid(x)


def _gelu_tanh(x):
    return 0.5 * x * (1.0 + jnp.tanh(0.7978845608028654 * (x + 0.044715 * (x * x * x))))


def _rmsnorm(x, g):
    return x * lax.rsqrt(jnp.mean(x * x, axis=-1, keepdims=True) + RMS_EPS) * g


def _resident(shape):
    nd = len(shape)
    return pl.BlockSpec(shape, lambda *_: (0,) * nd, pipeline_mode=pl.Buffered(1))


def _layer_resident(shape, layer):
    nd = len(shape)
    return pl.BlockSpec((None,) + tuple(shape[1:]), lambda *_: (layer,) + (0,) * (nd - 1),
                        pipeline_mode=pl.Buffered(1))


def _weight_halves(w, lead):
    k, n = w.shape[-2:]
    half = n // 2
    assert 2 * half == n and half % MXU_WIDTH == 0 and half % 1024 != 0
    block = (None,) * len(lead) + (k, half)
    return [(w, pl.BlockSpec(block, functools.partial(lambda part, *_: tuple(lead) + (0, part), part),
                             pipeline_mode=pl.Buffered(1)))
            for part in range(2)]


def _emit_interleaved(vector_tasks, matmul_tasks):
    n, m = len(vector_tasks), len(matmul_tasks)
    i = j = 0
    while i < n or j < m:
        if j >= m or (i < n and i * m <= j * n):
            vector_tasks[i]()
            i += 1
        else:
            matmul_tasks[j]()
            j += 1


def _ffn_kernel(x_ref, g_ref, win_ref, wout_ref, gfin_ref, o_ref, *, d_ff, ff_chunk, final_norm):
    x = x_ref[...]
    xn = _rmsnorm(x, g_ref[...]).astype(BF16)
    acc = jnp.zeros(x.shape, F32)
    for j in range(d_ff // ff_chunk):
        lo = j * ff_chunk
        gate = _dot(xn, win_ref[:, lo:lo + ff_chunk].astype(BF16))
        up = _dot(xn, win_ref[:, d_ff + lo:d_ff + lo + ff_chunk].astype(BF16))
        h = (_silu(gate) * up).astype(BF16)
        acc = acc + _dot(h, wout_ref[lo:lo + ff_chunk, :].astype(BF16))
    out = x + 0.5 * acc
    o_ref[...] = _rmsnorm(out, gfin_ref[...]) if final_norm else out


def _ffn(x2d, layer, g, w_in, w_out, g_final, *, tm, ff_chunk, final_norm):
    t, d = x2d.shape
    d_ff = w_out.shape[1]
    return pl.pallas_call(
        functools.partial(_ffn_kernel, d_ff=d_ff, ff_chunk=ff_chunk, final_norm=final_norm),
        out_shape=jax.ShapeDtypeStruct((t, d), F32),
        grid=(t // tm,),
        in_specs=[
            pl.BlockSpec((tm, d), lambda i: (i, 0)),
            _layer_resident(g.shape, layer),
            _layer_resident(w_in.shape, layer),
            _layer_resident(w_out.shape, layer),
            _resident(g_final.shape),
        ],
        out_specs=pl.BlockSpec((tm, d), lambda i: (i, 0)),
        compiler_params=pltpu.CompilerParams(
            dimension_semantics=("parallel",), vmem_limit_bytes=VMEM_LIMIT),
        name="ffn",
    )(x2d, g, w_in, w_out, g_final)


def _split3(x):
    hi = x.astype(BF16)
    r = x - hi.astype(F32)
    mid = r.astype(BF16)
    lo = (r - mid.astype(F32)).astype(BF16)
    return hi, mid, lo


def _block_mid(b, hs):
    c = b.shape[0]
    blk = 2 * hs

    def rows_every8(off):
        return jnp.concatenate(
            [jnp.broadcast_to(b[r + off:r + off + 1, :], (SUBLANES, b.shape[1]))
             for r in range(0, c, SUBLANES)], axis=0)

    if blk >= SUBLANES:
        return jnp.concatenate(
            [jnp.broadcast_to(b[r + hs - 1:r + hs, :], (blk, b.shape[1])) for r in range(0, c, blk)], axis=0)
    assert blk == 4
    row8 = lax.broadcasted_iota(jnp.int32, b.shape, 0) % SUBLANES
    return jnp.where(row8 < 4, rows_every8(1), rows_every8(5))


def _level_masks(c):
    ri = lax.broadcasted_iota(jnp.int32, (c, c), 0)
    ci = lax.broadcasted_iota(jnp.int32, (c, c), 1)
    masks = []
    hs = c // 2
    while hs >= 1:
        blk = 2 * hs
        masks.append(((ri // blk) == (ci // blk)) & ((ri % blk) >= hs) & ((ci % blk) < hs))
        hs //= 2
    return masks


def _hgrn2_scores(q, k, f, b):
    c = q.shape[0]
    row = lax.broadcasted_iota(jnp.int32, q.shape, 0)
    out = []
    hs = c // 2
    while hs >= 1:
        upper = (row % (2 * hs)) >= hs
        e = jnp.where(upper, f, 1.0) if hs == 1 else jnp.exp2(-jnp.abs(b - _block_mid(b, hs)))
        x = (jnp.where(upper, q, k) * e).astype(BF16)
        out.append(_dot_nt(x, x))
        hs //= 2
    return out


def _conv_unit(ext_ref, hist, w_ref, bias_ref, out_ref, *, row_block, lane_block):
    n_tap = w_ref.shape[0]
    base = hist - (n_tap - 1)
    groups = row_block // SUBLANES

    def unit(c0, lanes):
        sub = lax.broadcasted_iota(jnp.int32, (groups, SUBLANES, lane_block), 1)
        acc = jnp.broadcast_to(bias_ref[:, lanes][None], (groups, SUBLANES, lane_block))
        for r in range(SUBLANES):
            taps = [i for i in range(n_tap) if (base + i) % SUBLANES == r]
            if not taps:
                continue
            rows = row_block if r == 0 else row_block + SUBLANES
            p = None
            for i in taps:
                a0 = c0 + base + i - r
                win = ext_ref[a0:a0 + rows, lanes].reshape(rows // SUBLANES, SUBLANES, lane_block)
                term = win * w_ref[i, :, lanes][None]
                p = term if p is None else p + term
            if r:
                pr = pltpu.roll(p, SUBLANES - r, 1)
                p = jnp.where(sub < SUBLANES - r, pr[0:groups], pr[1:groups + 1])
            acc = acc + p
        out_ref[c0:c0 + row_block, lanes] = acc.reshape(row_block, lane_block)

    return unit


def _lru_scan(a_t, u_t, carry):
    ts, n = a_t.shape
    groups = ts // SUBLANES
    a3 = a_t.reshape(groups, SUBLANES, n)
    h3 = u_t.reshape(groups, SUBLANES, n)
    row = lax.broadcasted_iota(jnp.int32, a3.shape, 1)
    sh = 1
    while sh < SUBLANES:
        valid = row >= sh
        h_prev = jnp.where(valid, pltpu.roll(h3, sh, 1), 0.0)
        a_prev = jnp.where(valid, pltpu.roll(a3, sh, 1), 1.0)
        h3 = h3 + a3 * h_prev
        a3 = a3 * a_prev
        sh *= 2
    out = []
    for g in range(groups):
        hg = h3[g] + a3[g] * carry
        out.append(hg)
        carry = hg[SUBLANES - 1:SUBLANES, :]
    return jnp.concatenate(out, axis=0), carry


def _mixer_kernel(x_ref, nrm_ref, win0_ref, win1_ref, bin_ref, lbl_ref, hgn_ref, lcw_ref, lcb_ref, lgw_ref,
                  lgb_ref, lam_ref, cw_ref, cb_ref, lng_ref, lnb_ref,
                  wbr0_ref, wbr1_ref, wbr2_ref, wbr3_ref, wbr4_ref, wbr5_ref, wout0_ref, wout1_ref,
                  o_ref,
                  st_ref, hl_ref, lxe_ref, cue_ref, xn_ref, y16_ref, a16_ref,
                  q_ref, f_ref, v_ref, b_ref, y_ref, m_ref, cv_ref,
                  *, layer, n_heads):
    ts, d = x_ref.shape
    chunk = HG_CHUNK
    lru_hist, cv_hist = SUBLANES, 4 * SUBLANES
    panels = [slice(c, c + MXU_WIDTH) for c in range(0, d, MXU_WIDTH)]
    win_refs, wout_refs = (win0_ref, win1_ref), (wout0_ref, wout1_ref)
    wbr_refs = (wbr0_ref, wbr1_ref, wbr2_ref, wbr3_ref, wbr4_ref, wbr5_ref)

    @pl.when(pl.program_id(1) == 0)
    def _():
        st_ref[...] = jnp.zeros(st_ref.shape, F32)
        hl_ref[...] = jnp.zeros(hl_ref.shape, F32)
        lxe_ref[0:lru_hist, :] = jnp.zeros((lru_hist, d), F32)
        cue_ref[0:cv_hist, :] = jnp.zeros((cv_hist, d), F32)

    def stage(body):
        body()

    xn_ref[...] = _rmsnorm(x_ref[...], nrm_ref[...]).astype(BF16)

    def proj(i, p):
        c = i * d + p.start
        return _dot(xn_ref[...], _panel(win_refs, c)) + bin_ref[:, c:c + MXU_WIDTH]

    def branch_proj(a16_ref_, i, p):
        return _dot(a16_ref_[...], _panel(wbr_refs, i * d + p.start))

    lg = lbl_ref[...]
    ex = jnp.exp(lg - jnp.max(lg, axis=0, keepdims=True))
    sm = ex / jnp.sum(ex, axis=0, keepdims=True)
    lb = jnp.zeros((1, d), F32)
    for i in range(1, layer + 1):
        lb = lb + sm[i:i + 1, :]

    for p in panels:
        lxe_ref[lru_hist:lru_hist + ts, p] = proj(4, p)
    xb_ref = y_ref
    conv_rows = ts // 2
    lru_conv = _conv_unit(lxe_ref, lru_hist, lcw_ref, lcb_ref, xb_ref, row_block=conv_rows, lane_block=CONV_LANES)
    cv_conv = _conv_unit(cue_ref, cv_hist, cw_ref, cb_ref, cv_ref, row_block=conv_rows, lane_block=CONV_LANES)
    conv_cols = [slice(l0, l0 + CONV_LANES) for l0 in range(0, d, CONV_LANES)]

    def lru_conv_task(lanes):
        lru_conv(0, lanes)
        lru_conv(conv_rows, lanes)

    def glu_task(p):
        cue_ref[cv_hist:cv_hist + ts, p] = proj(6, p) * _sigmoid(proj(7, p))

    @stage
    def _():
        _emit_interleaved([functools.partial(lru_conv_task, lanes) for lanes in conv_cols],
                          [functools.partial(glu_task, p) for p in panels])
        lxe_ref[0:lru_hist, :] = lxe_ref[ts:ts + lru_hist, :]

    def q_task(p):
        q_ref[:, p] = _silu(proj(0, p))

    def f_task(p):
        f_ref[:, p] = lb[:, p] + (1.0 - lb[:, p]) * _sigmoid(proj(1, p))

    def v_task(p):
        v_ref[:, p] = proj(2, p)

    def lgate_task(p):
        m_ref[:, p] = _gelu_tanh(proj(5, p))

    lam = -lam_ref[...]
    softplus = jnp.maximum(lam, 0.0) + jnp.log(1.0 + jnp.exp(-jnp.abs(lam)))

    def lru_task(h):
        cols = slice(h * HEAD_DIM, (h + 1) * HEAD_DIM)
        xb = xb_ref[:, cols]
        gates = _dot(xb.astype(BF16), lgw_ref[h])
        gr = gates[:, :HEAD_DIM] + lgb_ref[0:1, cols]
        gi = gates[:, HEAD_DIM:] + lgb_ref[1:2, cols]
        a_t = jnp.exp((-LRU_C) * _sigmoid(gr) * softplus[:, cols])
        u_t = jnp.sqrt(1.0 - a_t * a_t) * (_sigmoid(gi) * xb)
        hh, last = _lru_scan(a_t, u_t, hl_ref[:, cols])
        hl_ref[:, cols] = last
        a16_ref[:, cols] = (hh * m_ref[:, cols]).astype(BF16)

    vector_tasks = [functools.partial(cv_conv, 0, lanes) for lanes in conv_cols]
    vector_tasks += [functools.partial(lru_task, h) for h in range(d // HEAD_DIM)]
    @stage
    def _():
        _emit_interleaved(
            vector_tasks, [functools.partial(t, p) for t in (lgate_task, q_task, f_task, v_task) for p in panels])

    def cumsum_chunk(c):
        tri = (lax.broadcasted_iota(jnp.int32, (chunk, chunk), 0)
               >= lax.broadcasted_iota(jnp.int32, (chunk, chunk), 1)).astype(BF16)
        rows = slice(c * chunk, (c + 1) * chunk)
        g_hi, g_mid, g_lo = _split3(jnp.log2(f_ref[rows, :]))
        b_ref[rows, :] = _dot(tri, g_hi) + _dot(tri, g_mid) + _dot(tri, g_lo)


    def hgrn2_chunk(c):
        rows = slice(c * chunk, (c + 1) * chunk)
        masks = _level_masks(chunk)
        heads = [slice(h * HEAD_DIM, (h + 1) * HEAD_DIM) for h in range(n_heads)]
        scores, carried = [], []
        for h, cols in enumerate(heads):
            q, f, v, b = q_ref[rows, cols], f_ref[rows, cols], v_ref[rows, cols], b_ref[rows, cols]
            k = 1.0 - f
            scores.append(_hgrn2_scores(q, k, f, b))
            st = st_ref[h]
            carried.append(_dot_nt((q * jnp.exp2(b)).astype(BF16), st.astype(BF16)))
            b_last = b[chunk - 1:chunk, :]
            kd = (k * jnp.exp2(b_last - b)).astype(BF16)
            st_ref[h] = st * jnp.exp2(b_last) + _dot_tn(v.astype(BF16), kd)
        for h, cols in enumerate(heads):
            q, f, v = q_ref[rows, cols], f_ref[rows, cols], v_ref[rows, cols]
            a = jnp.zeros((chunk, chunk), F32)
            for m, al in zip(masks, scores[h]):
                a = jnp.where(m, al, a)
            o = _dot(a.astype(BF16), v.astype(BF16)) + carried[h]
            o = o + jnp.sum(q * (1.0 - f), axis=-1, keepdims=True) * v
            y_ref[rows, cols] = _rmsnorm(o, hgn_ref[...])

    @stage
    def _():
        for c in range(ts // chunk):
            cumsum_chunk(c)
        for c in range(ts // chunk):
            hgrn2_chunk(c)

    ga_ref, gb_ref, gc_ref = q_ref, f_ref, v_ref

    def yhg_task(p):
        y16_ref[:, p] = (y_ref[:, p] * _silu(proj(3, p))).astype(BF16)

    def gate_task(ref, i, p):
        ref[:, p] = _sigmoid(proj(i, p))

    def merge_ab_task(p):
        m_ref[:, p] = (ga_ref[:, p] * branch_proj(y16_ref, 0, p)
                       + gb_ref[:, p] * branch_proj(a16_ref, 1, p))

    matmul_tasks = [functools.partial(yhg_task, p) for p in panels]
    matmul_tasks += [functools.partial(gate_task, ga_ref, 8, p) for p in panels]
    matmul_tasks += [functools.partial(gate_task, gb_ref, 9, p) for p in panels]
    matmul_tasks += [functools.partial(merge_ab_task, p) for p in panels]
    matmul_tasks += [functools.partial(gate_task, gc_ref, 10, p) for p in panels]

    @stage
    def _():
        _emit_interleaved(
            [functools.partial(cv_conv, conv_rows, lanes) for lanes in conv_cols],
            matmul_tasks)
        cue_ref[0:cv_hist, :] = cue_ref[ts:ts + cv_hist, :]

    def ln_task(r0):
        rows = slice(r0, r0 + conv_rows)
        cv = cv_ref[rows, :]
        mu = jnp.mean(cv, axis=-1, keepdims=True)
        xc = cv - mu
        ln = xc * lax.rsqrt(jnp.mean(xc * xc, axis=-1, keepdims=True) + LN_EPS) * lng_ref[...] + lnb_ref[...]
        xn_ref[rows, :] = _silu(ln).astype(BF16)

    @stage
    def _():
        for r0 in range(0, ts, conv_rows):
            ln_task(r0)
        for p in panels:
            y16_ref[:, p] = (m_ref[:, p] + gc_ref[:, p] * branch_proj(xn_ref, 2, p)).astype(BF16)
        for p in panels:
            o_ref[:, p] = x_ref[:, p] + _dot(y16_ref[...], _panel(wout_refs, p.start))


def _mixer(x, layer, params, *, ts):
    b, s, d = x.shape
    n_heads = d // HEAD_DIM
    pairs = [p if isinstance(p, tuple) else (p, _layer_resident(p.shape, layer)) for p in params]
    args = [x] + [a for a, _ in pairs]
    in_specs = [pl.BlockSpec((None, ts, d), lambda bi, si: (bi, si, 0))] + [spec for _, spec in pairs]
    return pl.pallas_call(
        functools.partial(_mixer_kernel, layer=layer, n_heads=n_heads),
        out_shape=jax.ShapeDtypeStruct((b, s, d), F32),
        grid=(b, s // ts),
        in_specs=in_specs,
        out_specs=pl.BlockSpec((None, ts, d), lambda bi, si: (bi, si, 0)),
        scratch_shapes=[
            pltpu.VMEM((n_heads, HEAD_DIM, HEAD_DIM), F32),
            pltpu.VMEM((1, d), F32),
            pltpu.VMEM((ts + SUBLANES, d), F32),
            pltpu.VMEM((ts + 4 * SUBLANES, d), F32),
            pltpu.VMEM((ts, d), BF16),
            pltpu.VMEM((ts, d), BF16),
            pltpu.VMEM((ts, d), BF16),
            pltpu.VMEM((ts, d), F32),
            pltpu.VMEM((ts, d), F32),
            pltpu.VMEM((ts, d), F32),
            pltpu.VMEM((ts, d), F32),
            pltpu.VMEM((ts, d), F32),
            pltpu.VMEM((ts, d), F32),
            pltpu.VMEM((ts, d), F32),
        ],
        compiler_params=pltpu.CompilerParams(
            dimension_semantics=("parallel", "arbitrary"), vmem_limit_bytes=VMEM_LIMIT),
        name="mixer",
    )(*args)


def _tile(n, want):
    t = min(n, want)
    assert n % t == 0, (n, t)
    return t


def kernel(x, norm_ffn1, ffn1_w_in, ffn1_w_out, norm_mix, w_in_mix, b_in_mix, hgrn_lb_logits, hg_norm,
           lru_conv_w, lru_conv_b, lru_gate_w, lru_gate_b, lru_lambda, cv_dw_w, cv_dw_b, cv_ln_g, cv_ln_b,
           w_branch, w_out_mix, norm_ffn2, ffn2_w_in, ffn2_w_out, norm_final):
    bsz, seq, d = x.shape
    depth = norm_ffn1.shape[0]
    t = bsz * seq
    tm = _tile(t, 512)
    ts = _tile(seq, 256)
    assert ts % HG_CHUNK == 0 and d % HEAD_DIM == 0 and d % MXU_WIDTH == 0
    d_ff = ffn1_w_out.shape[1]
    ff_chunk = 256 if d_ff % 256 == 0 else d_ff

    rows = lambda a: a[:, None, :]
    ffn1 = (rows(norm_ffn1), ffn1_w_in, ffn1_w_out)
    ffn2 = (rows(norm_ffn2), ffn2_w_in, ffn2_w_out)
    lgw = jnp.concatenate([lru_gate_w[:, 0], lru_gate_w[:, 1]], axis=-1).astype(BF16)
    taps8 = lambda w: jnp.broadcast_to(w[:, :, None, :], w.shape[:2] + (SUBLANES, w.shape[2]))
    w_in16, w_br16, w_out16 = w_in_mix.astype(BF16), w_branch.astype(BF16), w_out_mix.astype(BF16)

    def mixer_params(l):
        branch_halves = [h for i in range(w_br16.shape[1]) for h in _weight_halves(w_br16, (l, i))]
        return [rows(norm_mix), *_weight_halves(w_in16, (l,)), rows(b_in_mix),
                (hgrn_lb_logits, _resident(hgrn_lb_logits.shape)), rows(hg_norm),
                taps8(lru_conv_w), rows(lru_conv_b), lgw, lru_gate_b, rows(lru_lambda), taps8(cv_dw_w),
                rows(cv_dw_b), rows(cv_ln_g), rows(cv_ln_b), *branch_halves, *_weight_halves(w_out16, (l,))]

    g_final = norm_final.reshape(1, d)
    for l in range(depth):
        x = _ffn(x.reshape(t, d), l, *ffn1, g_final, tm=tm, ff_chunk=ff_chunk, final_norm=False)
        x = _mixer(x.reshape(bsz, seq, d), l, mixer_params(l), ts=ts)
        x = _ffn(x.reshape(t, d), l, *ffn2, g_final, tm=tm, ff_chunk=ff_chunk, final_norm=(l == depth - 1))
    return x.reshape(bsz, seq, d)
```
